```python
import math
import jax, jax.numpy as jnp
from jax import lax
import numpy as np

D_MODEL = 2048
BATCH = 32
SEQ = 256
DEPTH = 4
DEC_BATCH = 8
DEC_SEQ = 1024
PAST_LEN = 512

GRID_W = 64
N_HEADS = 16
NOPE_DIM = 128
ROPE_DIM = 64
V_DIM = 128
QK_DIM = NOPE_DIM + ROPE_DIM
Q_LORA = 512
KV_LORA = 512
ROPE_PAIRS = ROPE_DIM // 4
ROPE_BASE = 10000.0
Q_BLOCK = 128
CONV_W = 1024
CONV_K = 31
CONV_PAD = (CONV_K - 1) // 2
SSM_W = 1024
SSM_GROUP_CH = 16
SSM_GROUPS = SSM_W // SSM_GROUP_CH
SSM_STATE = 64
D_FF = 4 * D_MODEL
N_BRANCH = 3
IN_COLS = Q_LORA + KV_LORA + ROPE_DIM + 2 * CONV_W + SSM_W + N_BRANCH * D_MODEL
EPS = 1e-6

kernel_name = 'hybrid_mla_conv_s5_diffusion_step'


def _rms(x, g):
    xf = x.astype(jnp.float32)
    y = xf * lax.rsqrt(jnp.mean(xf * xf, axis=-1, keepdims=True) + EPS)
    return (y * g.astype(jnp.float32)).astype(x.dtype)


def _axial_rope_tables(n_tok):
    rows = n_tok // GRID_W
    row = jnp.repeat(jnp.arange(rows), GRID_W).astype(jnp.float32)
    col = jnp.tile(jnp.arange(GRID_W), rows).astype(jnp.float32)
    inv = ROPE_BASE ** (-jnp.arange(ROPE_PAIRS, dtype=jnp.float32) / ROPE_PAIRS)
    ang = jnp.stack([row[:, None] * inv, col[:, None] * inv], axis=1)
    return jnp.cos(ang), jnp.sin(ang)


def _rope(x, cos, sin):
    extra = x.ndim - 3
    shp = (cos.shape[0],) + (1,) * extra + (2, ROPE_PAIRS)
    cs, sn = cos.reshape(shp), sin.reshape(shp)
    xf = x.astype(jnp.float32).reshape(x.shape[:-1] + (2, 2, ROPE_PAIRS))
    x1, x2 = xf[..., 0, :], xf[..., 1, :]
    out = jnp.stack([x1 * cs - x2 * sn, x2 * cs + x1 * sn], axis=-2)
    return out.reshape(x.shape).astype(x.dtype)


def _attend(q, k, v):
    b, sq, h, dq = q.shape
    nb = sq // Q_BLOCK
    qb = q.reshape(b, nb, Q_BLOCK, h, dq).swapaxes(0, 1)
    scale = 1.0 / math.sqrt(QK_DIM)

    def one(qi):
        s = jnp.einsum('bqhd,bkhd->bhqk', qi, k).astype(jnp.float32) * scale
        pr = jax.nn.softmax(s, axis=-1).astype(v.dtype)
        return jnp.einsum('bhqk,bkhd->bqhd', pr, v)

    o = lax.map(one, qb)
    return o.swapaxes(0, 1).reshape(b, sq, h, v.shape[-1])


def _conv_branch(z2, p):
    a, b = jnp.split(z2, 2, axis=-1)
    z = a * jax.nn.sigmoid(b)
    z = lax.conv_general_dilated(z, p['conv_w'][:, None, :], window_strides=(1,),
                                 padding=[(CONV_PAD, CONV_PAD)],
                                 dimension_numbers=('NWC', 'WIO', 'NWC'),
                                 feature_group_count=CONV_W) + p['conv_b']
    zf = z.astype(jnp.float32)
    mu = jnp.mean(zf, axis=-1, keepdims=True)
    var = jnp.mean(jnp.square(zf - mu), axis=-1, keepdims=True)
    z = ((zf - mu) * lax.rsqrt(var + EPS) * p['conv_ln_g'].astype(jnp.float32)
         + p['conv_ln_b'].astype(jnp.float32)).astype(z2.dtype)
    return jax.nn.silu(z) @ p['w_conv_o']


def _cmul_combine(e1, e2):
    a1r, a1i, b1r, b1i = e1
    a2r, a2i, b2r, b2i = e2
    return (a1r * a2r - a1i * a2i, a1r * a2i + a1i * a2r,
            a2r * b1r - a2i * b1i + b2r, a2r * b1i + a2i * b1r + b2i)


def _ssm_branch(u, p, h0_re, h0_im):
    bsz, L, _ = u.shape
    uf = u.astype(jnp.float32).reshape(bsz, L, SSM_GROUPS, SSM_GROUP_CH)
    y = p['ssm_d'].astype(jnp.float32).reshape(SSM_GROUPS, SSM_GROUP_CH) * uf
    fins_re, fins_im = [], []
    for d in range(2):
        a_re = p['ssm_a_re'][d].astype(jnp.float32)
        a_im = p['ssm_a_im'][d].astype(jnp.float32)
        dt = jnp.exp(p['ssm_log_dt'][d].astype(jnp.float32))[:, None]
        mag = jnp.exp(a_re * dt)
        lb_re, lb_im = mag * jnp.cos(a_im * dt), mag * jnp.sin(a_im * dt)
        den = a_re * a_re + a_im * a_im
        nr = lb_re - 1.0
        f_re = (nr * a_re + lb_im * a_im) / den
        f_im = (lb_im * a_re - nr * a_im) / den
        b_re = p['ssm_b_re'][d].astype(jnp.float32)
        b_im = p['ssm_b_im'][d].astype(jnp.float32)
        bb_re = f_re[..., None] * b_re - f_im[..., None] * b_im
        bb_im = f_re[..., None] * b_im + f_im[..., None] * b_re
        ud = uf if d == 0 else jnp.flip(uf, axis=1)
        bu_re = jnp.einsum('blgc,gpc->blgp', ud, bb_re)
        bu_im = jnp.einsum('blgc,gpc->blgp', ud, bb_im)
        if h0_re is not None:
            hr = h0_re[:, d].astype(jnp.float32)
            hi = h0_im[:, d].astype(jnp.float32)
            bu_re = bu_re.at[:, 0].add(lb_re * hr - lb_im * hi)
            bu_im = bu_im.at[:, 0].add(lb_re * hi + lb_im * hr)
        ar = jnp.broadcast_to(lb_re, bu_re.shape)
        ai = jnp.broadcast_to(lb_im, bu_im.shape)
        _, _, xr, xi = lax.associative_scan(_cmul_combine, (ar, ai, bu_re, bu_im), axis=1)
        fins_re.append(xr[:, -1])
        fins_im.append(xi[:, -1])
        c_re = p['ssm_c_re'][d].astype(jnp.float32)
        c_im = p['ssm_c_im'][d].astype(jnp.float32)
        yd = jnp.einsum('blgp,gcp->blgc', xr, c_re) - jnp.einsum('blgp,gcp->blgc', xi, c_im)
        y = y + (yd if d == 0 else jnp.flip(yd, axis=1))
    y = y.reshape(bsz, L, SSM_W).astype(u.dtype)
    g = jax.nn.gelu(y)
    out = g * jax.nn.sigmoid(g @ p['w_glu'] + p['b_glu'])
    return out @ p['w_ssm_o'], jnp.stack(fins_re, axis=1), jnp.stack(fins_im, axis=1)


def _block(x, mod, p, lat):
    shift1, scale1, gate1, shift2, scale2, gate2 = jnp.split(mod, 6, axis=-1)
    bsz, L, _ = x.shape
    h = _rms(x, p['g_mix_pre']) * (1 + scale1) + shift1
    proj = h @ p['w_in']
    i1 = Q_LORA
    i2 = i1 + KV_LORA
    i3 = i2 + ROPE_DIM
    i4 = i3 + 2 * CONV_W
    i5 = i4 + SSM_W
    c_q, c_kv, k_pe, conv_in, ssm_in, gate_in = jnp.split(proj, [i1, i2, i3, i4, i5], axis=-1)
    q = (_rms(c_q, p['g_q']) @ p['w_uq']).reshape(bsz, L, N_HEADS, QK_DIM)
    ckv = _rms(c_kv, p['g_kv'])
    if lat is None:
        ckv_keys, kpe_keys = ckv, k_pe
        h0_re = h0_im = None
    else:
        cos, sin, ckv_ctx, kpe_ctx, h0_re, h0_im = lat
        q = jnp.concatenate([q[..., :NOPE_DIM], _rope(q[..., NOPE_DIM:], cos, sin)], axis=-1)
        ckv_keys = jnp.concatenate([ckv, ckv_ctx.astype(ckv.dtype)], axis=1)
        kpe_keys = jnp.concatenate([_rope(k_pe, cos, sin), kpe_ctx.astype(k_pe.dtype)], axis=1)
    kv = (ckv_keys @ p['w_ukv']).reshape(bsz, -1, N_HEADS, NOPE_DIM + V_DIM)
    sk = kv.shape[1]
    k = jnp.concatenate([kv[..., :NOPE_DIM],
                         jnp.broadcast_to(kpe_keys[:, :, None, :], (bsz, sk, N_HEADS, ROPE_DIM))], axis=-1)
    v = kv[..., NOPE_DIM:]
    attn = _attend(q, k, v).reshape(bsz, L, N_HEADS * V_DIM) @ p['w_attn_o']
    conv = _conv_branch(conv_in, p)
    ssm, fin_re, fin_im = _ssm_branch(ssm_in, p, h0_re, h0_im)
    g_a, g_c, g_s = jnp.split(jax.nn.sigmoid(gate_in), N_BRANCH, axis=-1)
    mix = (g_a * attn + g_c * conv + g_s * ssm) @ p['w_out']
    x = x + gate1 * _rms(mix, p['g_mix_post'])
    h2 = _rms(x, p['g_mlp_pre']) * (1 + scale2) + shift2
    f = jnp.square(jax.nn.relu(h2 @ p['w_ff1'])) @ p['w_ff2']
    x = x + gate2 * _rms(f, p['g_mlp_post'])
    return x, ckv, k_pe, fin_re, fin_im


def setup_inputs(seed: int = 0) -> dict:
    key = jax.random.key(seed)
    ks = iter(jax.random.split(key, 64))

    def nrm(shape, s):
        return jax.random.normal(next(ks), shape, jnp.float32) * s

    def gain(shape):
        return 1.0 + nrm(shape, 0.01)

    L = DEPTH
    n_idx = jnp.arange(SSM_STATE, dtype=jnp.float32)
    return {
        'x_prompt': nrm((BATCH, SEQ, D_MODEL), 1.0),
        'x_sample': nrm((DEC_BATCH, DEC_SEQ, D_MODEL), 1.0),
        'cache_ckv': nrm((DEC_BATCH, DEPTH, PAST_LEN, KV_LORA), 1.0),
        'cache_kpe': nrm((DEC_BATCH, DEPTH, PAST_LEN, ROPE_DIM), 1.0),
        'state_ssm_re': nrm((DEC_BATCH, DEPTH, 2, SSM_GROUPS, SSM_STATE), 0.5),
        'state_ssm_im': nrm((DEC_BATCH, DEPTH, 2, SSM_GROUPS, SSM_STATE), 0.5),
        'c': nrm((DEC_BATCH, D_MODEL), 1.0),
        'c_ctx': nrm((D_MODEL,), 1.0),
        'w_mod': nrm((L, D_MODEL, 6 * D_MODEL), 0.5 * D_MODEL ** -0.5),
        'b_mod': nrm((L, 6 * D_MODEL), 0.01),
        'g_mix_pre': gain((L, D_MODEL)),
        'g_mix_post': gain((L, D_MODEL)),
        'g_mlp_pre': gain((L, D_MODEL)),
        'g_mlp_post': gain((L, D_MODEL)),
        'w_in': nrm((L, D_MODEL, IN_COLS), D_MODEL ** -0.5),
        'g_q': gain((L, Q_LORA)),
        'w_uq': nrm((L, Q_LORA, N_HEADS * QK_DIM), Q_LORA ** -0.5),
        'g_kv': gain((L, KV_LORA)),
        'w_ukv': nrm((L, KV_LORA, N_HEADS * (NOPE_DIM + V_DIM)), KV_LORA ** -0.5),
        'w_attn_o': nrm((L, N_HEADS * V_DIM, D_MODEL), (N_HEADS * V_DIM) ** -0.5),
        'conv_w': nrm((L, CONV_K, CONV_W), CONV_K ** -0.5),
        'conv_b': nrm((L, CONV_W), 0.01),
        'conv_ln_g': gain((L, CONV_W)),
        'conv_ln_b': nrm((L, CONV_W), 0.01),
        'w_conv_o': nrm((L, CONV_W, D_MODEL), CONV_W ** -0.5),
        'ssm_a_re': -0.5 + nrm((L, 2, SSM_GROUPS, SSM_STATE), 0.01),
        'ssm_a_im': math.pi * n_idx + nrm((L, 2, SSM_GROUPS, SSM_STATE), 0.01),
        'ssm_log_dt': jax.random.uniform(next(ks), (L, 2, SSM_GROUPS), jnp.float32,
                                         minval=math.log(1e-3), maxval=math.log(1e-1)),
        'ssm_b_re': nrm((L, 2, SSM_GROUPS, SSM_STATE, SSM_GROUP_CH), (2 * SSM_GROUP_CH) ** -0.5),
        'ssm_b_im': nrm((L, 2, SSM_GROUPS, SSM_STATE, SSM_GROUP_CH), (2 * SSM_GROUP_CH) ** -0.5),
        'ssm_c_re': nrm((L, 2, SSM_GROUPS, SSM_GROUP_CH, SSM_STATE), (2 * SSM_STATE) ** -0.5),
        'ssm_c_im': nrm((L, 2, SSM_GROUPS, SSM_GROUP_CH, SSM_STATE), (2 * SSM_STATE) ** -0.5),
        'ssm_d': nrm((L, SSM_W), 1.0),
        'w_glu': nrm((L, SSM_W, SSM_W), SSM_W ** -0.5),
        'b_glu': nrm((L, SSM_W), 0.01),
        'w_ssm_o': nrm((L, SSM_W, D_MODEL), SSM_W ** -0.5),
        'w_out': nrm((L, D_MODEL, D_MODEL), D_MODEL ** -0.5),
        'w_ff1': nrm((L, D_MODEL, D_FF), D_MODEL ** -0.5),
        'w_ff2': nrm((L, D_FF, D_MODEL), D_FF ** -0.5),
    }


def reference(x_prompt, x_sample, cache_ckv, cache_kpe, state_ssm_re, state_ssm_im, c, c_ctx,
              w_mod, b_mod, g_mix_pre, g_mix_post, g_mlp_pre, g_mlp_post, w_in, g_q, w_uq,
              g_kv, w_ukv, w_attn_o, conv_w, conv_b, conv_ln_g, conv_ln_b, w_conv_o,
              ssm_a_re, ssm_a_im, ssm_log_dt, ssm_b_re, ssm_b_im, ssm_c_re, ssm_c_im, ssm_d,
              w_glu, b_glu, w_ssm_o, w_out, w_ff1, w_ff2):
    cos, sin = _axial_rope_tables(x_sample.shape[1])
    xp, xs = x_prompt, x_sample
    ckvs, kpes, srs, sis = [], [], [], []
    for l in range(DEPTH):
        p = {
            'g_mix_pre': g_mix_pre[l], 'g_mix_post': g_mix_post[l],
            'g_mlp_pre': g_mlp_pre[l], 'g_mlp_post': g_mlp_post[l],
            'w_in': w_in[l], 'g_q': g_q[l], 'w_uq': w_uq[l], 'g_kv': g_kv[l],
            'w_ukv': w_ukv[l], 'w_attn_o': w_attn_o[l],
            'conv_w': conv_w[l], 'conv_b': conv_b[l], 'conv_ln_g': conv_ln_g[l],
            'conv_ln_b': conv_ln_b[l], 'w_conv_o': w_conv_o[l],
            'ssm_a_re': ssm_a_re[l], 'ssm_a_im': ssm_a_im[l], 'ssm_log_dt': ssm_log_dt[l],
            'ssm_b_re': ssm_b_re[l], 'ssm_b_im': ssm_b_im[l],
            'ssm_c_re': ssm_c_re[l], 'ssm_c_im': ssm_c_im[l], 'ssm_d': ssm_d[l],
            'w_glu': w_glu[l], 'b_glu': b_glu[l], 'w_ssm_o': w_ssm_o[l],
            'w_out': w_out[l], 'w_ff1': w_ff1[l], 'w_ff2': w_ff2[l],
        }
        mod_ctx = (jax.nn.silu(c_ctx) @ w_mod[l] + b_mod[l])[None, None, :]
        mod_lat = (jax.nn.silu(c) @ w_mod[l] + b_mod[l])[:, None, :]
        xp, ckv, kpe, fr, fi = _block(xp, mod_ctx, p, None)
        ckvs.append(ckv)
        kpes.append(kpe)
        srs.append(fr)
        sis.append(fi)
        lat = (cos, sin, cache_ckv[:, l], cache_kpe[:, l], state_ssm_re[:, l], state_ssm_im[:, l])
        xs = _block(xs, mod_lat, p, lat)[0]
    new_ckv = jnp.stack(ckvs, axis=1)
    new_kpe = jnp.stack(kpes, axis=1)
    new_ssm_re = jnp.stack(srs, axis=1)
    new_ssm_im = jnp.stack(sis, axis=1)
    return (xp, xs, new_ckv, new_kpe, new_ssm_re, new_ssm_im)
```

```python
import functools
import math

import jax
import jax.numpy as jnp
from jax import lax
from jax.experimental import pallas as pl
from jax.experimental.pallas import tpu as pltpu

N_HEADS = 16
NOPE_DIM = 128
ROPE_DIM = 64
V_DIM = 128
GRID_W = 64
ROPE_BASE = 10000.0
SSM_GROUP_CH = 16
SSM_STATE = 64
EPS = 1e-6

HEAD_W = NOPE_DIM + 2 * ROPE_DIM
SSM_CHUNK = 16
V7X_VMEM_LIMIT_BYTES = 56 * 1024 * 1024

F32 = jnp.float32
BF16 = jnp.bfloat16


def _cparams(*sem):
    return pltpu.CompilerParams(dimension_semantics=sem, vmem_limit_bytes=V7X_VMEM_LIMIT_BYTES)


def _tile(n, target):
    if n <= target:
        return n
    for t in range(target, 7, -1):
        if n % t == 0 and t % 8 == 0:
            return t
    return n


def _rms(x, g):
    return x * lax.rsqrt(jnp.mean(x * x, axis=-1, keepdims=True) + EPS) * g


def _rmsmod(x, g, scale, shift):
    return _rms(x, g) * (1.0 + scale) + shift


def _mod_kernel(c_ref, w_ref, b_ref, o_ref):
    h = jax.nn.silu(c_ref[...]).astype(BF16)
    o_ref[...] = jnp.dot(h, w_ref[...], preferred_element_type=F32) + b_ref[...]


def _modulation(cond, w_mod, b_mod):
    ly, d, n = w_mod.shape
    r = cond.shape[0]
    tn = _tile(n, 1536)
    return pl.pallas_call(
        _mod_kernel,
        grid=(ly, n // tn),
        in_specs=[
            pl.BlockSpec((r, d), lambda l, j: (0, 0)),
            pl.BlockSpec((None, d, tn), lambda l, j: (l, 0, j)),
            pl.BlockSpec((None, 1, tn), lambda l, j: (l, 0, j)),
        ],
        out_specs=pl.BlockSpec((None, r, tn), lambda l, j: (l, 0, j)),
        out_shape=jax.ShapeDtypeStruct((ly, r, n), F32),
        compiler_params=_cparams("parallel", "parallel"),
        name="modulation",
    )(cond, w_mod, b_mod)


def _normmm_kernel(x_ref, g_ref, sc_ref, sh_ref, *rest, n_w, epilogue):
    w_refs, o_ref, h_ref = rest[:n_w], rest[n_w], rest[n_w + 1]

    @pl.when(pl.program_id(1) == 0)
    def _():
        h_ref[...] = _rmsmod(x_ref[...], g_ref[...], sc_ref[0], sh_ref[0]).astype(BF16)

    h = h_ref[...]
    a = jnp.dot(h, w_refs[0][...], preferred_element_type=F32)
    if epilogue == "glu":
        a = a * jax.nn.sigmoid(jnp.dot(h, w_refs[1][...], preferred_element_type=F32))
    elif epilogue == "sigmoid":
        a = jax.nn.sigmoid(a)
    elif epilogue == "relu2":
        a = jnp.square(jnp.maximum(a, 0.0))
    o_ref[...] = a.astype(o_ref.dtype)


def _normmm(x, g, scale, shift, ws, layer, seq_len, epilogue, tm=512, tn=512):
    t, d = x.shape
    n = ws[0].shape[2]
    tm = _tile(seq_len, tm)
    tn = _tile(n, tn)
    per_seq = seq_len // tm
    nb = scale.shape[0]
    bidx = (lambda i: i // per_seq) if nb > 1 else (lambda i: 0)
    w_spec = pl.BlockSpec((None, d, tn), lambda i, j: (layer, 0, j))
    return pl.pallas_call(
        functools.partial(_normmm_kernel, n_w=len(ws), epilogue=epilogue),
        grid=(t // tm, n // tn),
        in_specs=[
            pl.BlockSpec((tm, d), lambda i, j: (i, 0)),
            pl.BlockSpec((None, 1, d), lambda i, j: (layer, 0, 0)),
            pl.BlockSpec((1, 1, d), lambda i, j: (bidx(i), 0, 0)),
            pl.BlockSpec((1, 1, d), lambda i, j: (bidx(i), 0, 0)),
        ] + [w_spec] * len(ws),
        out_specs=pl.BlockSpec((tm, tn), lambda i, j: (i, j)),
        out_shape=jax.ShapeDtypeStruct((t, n), BF16),
        scratch_shapes=[pltpu.VMEM((tm, d), BF16)],
        compiler_params=_cparams("parallel", "arbitrary"),
        name="normmm_" + epilogue,
    )(x, g, scale, shift, *ws)


def _attn_in_kernel(x_ref, g_ref, sc_ref, sh_ref, wa_ref, gq_ref, gkv_ref, wuq_ref, cc_ref, ss_ref,
                    q_ref, ckv_ref, kpe_ref, *, q_lora, kv_lora):
    h = _rmsmod(x_ref[...], g_ref[...], sc_ref[0], sh_ref[0]).astype(BF16)
    a = jnp.dot(h, wa_ref[...], preferred_element_type=F32)
    cq = _rms(a[:, :q_lora], gq_ref[...]).astype(BF16)
    ckv_ref[...] = _rms(a[:, q_lora:q_lora + kv_lora], gkv_ref[...])
    kpe_ref[...] = a[:, q_lora + kv_lora:]
    q = jnp.dot(cq, wuq_ref[...], preferred_element_type=F32)
    cc, ss = cc_ref[...], ss_ref[...]
    for hd in range(N_HEADS):
        lo = hd * HEAD_W
        q_ref[:, lo:lo + NOPE_DIM] = q[:, lo:lo + NOPE_DIM].astype(BF16)
        v = q[:, lo + NOPE_DIM:lo + HEAD_W]
        q_ref[:, lo + NOPE_DIM:lo + HEAD_W] = (v * cc + pltpu.roll(v, ROPE_DIM, 1) * ss).astype(BF16)


def _attn_in(x, g, scale, shift, w_a, g_q, g_kv, w_uq, cc, ss, layer, seq_len, tm=256):
    t, d = x.shape
    na = w_a.shape[2]
    q_lora, kv_lora = g_q.shape[2], g_kv.shape[2]
    nq = w_uq.shape[2]
    tm = _tile(seq_len, tm)
    per_seq = seq_len // tm
    nb = scale.shape[0]
    bidx = (lambda i: i // per_seq) if nb > 1 else (lambda i: 0)
    ntab = cc.shape[0] // tm
    lsel = lambda i: (layer, 0, 0)
    return pl.pallas_call(
        functools.partial(_attn_in_kernel, q_lora=q_lora, kv_lora=kv_lora),
        grid=(t // tm,),
        in_specs=[
            pl.BlockSpec((tm, d), lambda i: (i, 0)),
            pl.BlockSpec((None, 1, d), lsel),
            pl.BlockSpec((1, 1, d), lambda i: (bidx(i), 0, 0)),
            pl.BlockSpec((1, 1, d), lambda i: (bidx(i), 0, 0)),
            pl.BlockSpec((None, d, na), lsel),
            pl.BlockSpec((None, 1, q_lora), lsel),
            pl.BlockSpec((None, 1, kv_lora), lsel),
            pl.BlockSpec((None, q_lora, nq), lsel),
            pl.BlockSpec((tm, 2 * ROPE_DIM), lambda i: (i % ntab, 0)),
            pl.BlockSpec((tm, 2 * ROPE_DIM), lambda i: (i % ntab, 0)),
        ],
        out_specs=[
            pl.BlockSpec((tm, nq), lambda i: (i, 0)),
            pl.BlockSpec((tm, kv_lora), lambda i: (i, 0)),
            pl.BlockSpec((tm, 2 * ROPE_DIM), lambda i: (i, 0)),
        ],
        out_shape=[
            jax.ShapeDtypeStruct((t, nq), BF16),
            jax.ShapeDtypeStruct((t, kv_lora), F32),
            jax.ShapeDtypeStruct((t, 2 * ROPE_DIM), F32),
        ],
        compiler_params=_cparams("parallel"),
        name="attn_in",
    )(x, g, scale, shift, w_a, g_q, g_kv, w_uq, cc, ss)


def _kv_build_kernel(ckv_ref, kpe_ref, cc_ref, ss_ref, wuk_ref, wuv_ref, k_ref, v_ref):
    c = ckv_ref[...].astype(BF16)
    kn = jnp.dot(c, wuk_ref[...], preferred_element_type=F32)
    v_ref[...] = jnp.dot(c, wuv_ref[...], preferred_element_type=F32).astype(BF16)
    kp = kpe_ref[...]
    kp = (kp * cc_ref[...] + pltpu.roll(kp, ROPE_DIM, 1) * ss_ref[...]).astype(BF16)
    for hd in range(N_HEADS):
        k_ref[:, hd * HEAD_W:hd * HEAD_W + NOPE_DIM] = kn[:, hd * NOPE_DIM:(hd + 1) * NOPE_DIM].astype(BF16)
        k_ref[:, hd * HEAD_W + NOPE_DIM:(hd + 1) * HEAD_W] = kp


def _kv_build(ckv, kpe2, cc, ss, w_uk, w_uv, layer, tm=256):
    t, kvl = ckv.shape
    tm = _tile(cc.shape[0], tm)
    ntab = cc.shape[0] // tm
    nk, nv = N_HEADS * HEAD_W, w_uv.shape[2]
    lsel = lambda i: (layer, 0, 0)
    return pl.pallas_call(
        _kv_build_kernel,
        grid=(t // tm,),
        in_specs=[
            pl.BlockSpec((tm, kvl), lambda i: (i, 0)),
            pl.BlockSpec((tm, 2 * ROPE_DIM), lambda i: (i, 0)),
            pl.BlockSpec((tm, 2 * ROPE_DIM), lambda i: (i % ntab, 0)),
            pl.BlockSpec((tm, 2 * ROPE_DIM), lambda i: (i % ntab, 0)),
            pl.BlockSpec((None, kvl, w_uk.shape[2]), lsel),
            pl.BlockSpec((None, kvl, nv), lsel),
        ],
        out_specs=[pl.BlockSpec((tm, nk), lambda i: (i, 0)), pl.BlockSpec((tm, nv), lambda i: (i, 0))],
        out_shape=[jax.ShapeDtypeStruct((t, nk), BF16), jax.ShapeDtypeStruct((t, nv), BF16)],
        compiler_params=_cparams("parallel"),
        name="kv_build",
    )(ckv, kpe2, cc, ss, w_uk, w_uv)


def _attn_kernel(q_ref, *rest, n_src, heads):
    kv_refs, o_ref = rest[:2 * n_src], rest[2 * n_src]
    scale = 1.0 / math.sqrt(NOPE_DIM + ROPE_DIM)
    for hd in range(heads):
        q = q_ref[:, hd * HEAD_W:(hd + 1) * HEAD_W]
        s = [lax.dot_general(q, kv_refs[2 * i][:, hd * HEAD_W:(hd + 1) * HEAD_W],
                             (((1,), (1,)), ((), ())), preferred_element_type=F32) * scale
             for i in range(n_src)]
        m = jnp.max(s[0], axis=-1, keepdims=True)
        for si in s[1:]:
            m = jnp.maximum(m, jnp.max(si, axis=-1, keepdims=True))
        acc, den = None, None
        for i in range(n_src):
            p = jnp.exp(s[i] - m)
            li = jnp.sum(p, axis=-1, keepdims=True)
            oi = jnp.dot(p.astype(BF16), kv_refs[2 * i + 1][:, hd * V_DIM:(hd + 1) * V_DIM],
                         preferred_element_type=F32)
            acc = oi if acc is None else acc + oi
            den = li if den is None else den + li
        o_ref[:, hd * V_DIM:(hd + 1) * V_DIM] = (acc / den).astype(BF16)


def _attention(q, kvs, heads_per_step, tq=256):
    b, sq, _ = q.shape
    tq = _tile(sq, tq)
    hb = heads_per_step
    in_specs = [pl.BlockSpec((None, tq, hb * HEAD_W), lambda bi, hi, qi: (bi, qi, hi))]
    args = [q]
    for k, v in kvs:
        sk = k.shape[1]
        in_specs.append(pl.BlockSpec((None, sk, hb * HEAD_W), lambda bi, hi, qi: (bi, 0, hi)))
        in_specs.append(pl.BlockSpec((None, sk, hb * V_DIM), lambda bi, hi, qi: (bi, 0, hi)))
        args += [k, v]
    return pl.pallas_call(
        functools.partial(_attn_kernel, n_src=len(kvs), heads=hb),
        grid=(b, N_HEADS // hb, sq // tq),
        in_specs=in_specs,
        out_specs=pl.BlockSpec((None, tq, hb * V_DIM), lambda bi, hi, qi: (bi, qi, hi)),
        out_shape=jax.ShapeDtypeStruct((b, sq, N_HEADS * V_DIM), BF16),
        compiler_params=_cparams("parallel", "parallel", "parallel"),
        name="attention",
    )(*args)


_CONV_ROWS = 32
_CONV_LANES = 256
_CONV_HALO = 16


def _conv_kernel(z_ref, w_ref, b_ref, lg_ref, lb_ref, o_ref, zp_ref, cv_ref, *, seq_len, taps):
    ch = z_ref.shape[-1]
    pad = (taps - 1) // 2
    zp_ref[0:_CONV_HALO, :] = jnp.zeros((_CONV_HALO, ch), F32)
    zp_ref[seq_len + _CONV_HALO:seq_len + 2 * _CONV_HALO, :] = jnp.zeros((_CONV_HALO, ch), F32)
    zp_ref[_CONV_HALO:seq_len + _CONV_HALO, :] = z_ref[...].astype(F32)
    n_lane = ch // _CONV_LANES

    def conv_step(it, carry):
        r0 = pl.multiple_of((it // n_lane) * _CONV_ROWS, _CONV_ROWS)
        c0 = pl.multiple_of((it % n_lane) * _CONV_LANES, _CONV_LANES)
        win = zp_ref[pl.ds(r0, _CONV_ROWS + 2 * _CONV_HALO), pl.ds(c0, _CONV_LANES)]
        w = w_ref[:, pl.ds(c0, _CONV_LANES)]
        acc = jnp.broadcast_to(b_ref[:, pl.ds(c0, _CONV_LANES)], (_CONV_ROWS, _CONV_LANES))
        for k in range(taps):
            off = _CONV_HALO - pad + k
            acc = acc + w[k:k + 1, :] * win[off:off + _CONV_ROWS, :]
        cv_ref[pl.ds(r0, _CONV_ROWS), pl.ds(c0, _CONV_LANES)] = acc
        return carry

    lax.fori_loop(0, (seq_len // _CONV_ROWS) * n_lane, conv_step, 0)

    def ln_step(it, carry):
        r0 = pl.multiple_of(it * _CONV_ROWS, _CONV_ROWS)
        v = cv_ref[pl.ds(r0, _CONV_ROWS), :]
        mu = jnp.mean(v, axis=-1, keepdims=True)
        vc = v - mu
        var = jnp.mean(vc * vc, axis=-1, keepdims=True)
        y = vc * lax.rsqrt(var + EPS) * lg_ref[...] + lb_ref[...]
        o_ref[pl.ds(r0, _CONV_ROWS), :] = jax.nn.silu(y).astype(BF16)
        return carry

    lax.fori_loop(0, seq_len // _CONV_ROWS, ln_step, 0)


def _conv_branch(z, conv_w, conv_b, ln_g, ln_b, layer, batch, seq_len):
    t, ch = z.shape
    taps = conv_w.shape[1]
    lsel = lambda b: (layer, 0, 0)
    return pl.pallas_call(
        functools.partial(_conv_kernel, seq_len=seq_len, taps=taps),
        grid=(batch,),
        in_specs=[
            pl.BlockSpec((seq_len, ch), lambda b: (b, 0)),
            pl.BlockSpec((None, taps, ch), lsel),
            pl.BlockSpec((None, 1, ch), lsel),
            pl.BlockSpec((None, 1, ch), lsel),
            pl.BlockSpec((None, 1, ch), lsel),
        ],
        out_specs=pl.BlockSpec((seq_len, ch), lambda b: (b, 0)),
        out_shape=jax.ShapeDtypeStruct((t, ch), BF16),
        scratch_shapes=[pltpu.VMEM((seq_len + 2 * _CONV_HALO, ch), F32), pltpu.VMEM((seq_len, ch), F32)],
        compiler_params=_cparams("parallel"),
        name="conv_ln_silu",
    )(z, conv_w, conv_b, ln_g, ln_b)


def _ssm_kernel(u_ref, h0_ref, t_ref, p_ref, r_ref, a_ref, y_ref, fin_ref, ds_ref, sin_ref, *,
                groups, n_chunks, batch):
    half = 2 * SSM_STATE
    lane = lax.broadcasted_iota(jnp.int32, (batch, half), 1)
    is_fwd = lane < SSM_STATE
    for gi in range(groups):
        u = u_ref[gi]
        ds_ref[...] = jnp.dot(u, p_ref[gi], preferred_element_type=F32)
        a_re, a_im = a_ref[gi, :, :half], a_ref[gi, :, half:]
        s_re, s_im = h0_ref[gi, :, :half], h0_ref[gi, :, half:]
        for c in range(n_chunks):
            rf = slice(c * batch, (c + 1) * batch)
            rb = slice((n_chunks - 1 - c) * batch, (n_chunks - c) * batch)
            sin_ref[rf, 0:SSM_STATE] = s_re[:, :SSM_STATE]
            sin_ref[rb, SSM_STATE:half] = s_re[:, SSM_STATE:]
            sin_ref[rf, half:half + SSM_STATE] = s_im[:, :SSM_STATE]
            sin_ref[rb, half + SSM_STATE:] = s_im[:, SSM_STATE:]
            d_re = jnp.where(is_fwd, ds_ref[rf, :half], ds_ref[rb, :half])
            d_im = jnp.where(is_fwd, ds_ref[rf, half:], ds_ref[rb, half:])
            s_re, s_im = (a_re * s_re - a_im * s_im + d_re, a_re * s_im + a_im * s_re + d_im)
        fin_ref[gi, :, :half] = s_re
        fin_ref[gi, :, half:] = s_im
        y = jnp.dot(u, t_ref[gi], preferred_element_type=F32)
        y = y + jnp.dot(sin_ref[...].astype(BF16), r_ref[gi], preferred_element_type=F32)
        y_ref[gi] = y.astype(BF16)


def _ssm_scan(u_r, h0, t_m, p_m, r_m, a_m, layer, batch, groups_per_step=4):
    g, m, w = u_r.shape
    n_chunks = m // batch
    gb = groups_per_step
    wsel = lambda i: (layer, i, 0, 0)
    return pl.pallas_call(
        functools.partial(_ssm_kernel, groups=gb, n_chunks=n_chunks, batch=batch),
        grid=(g // gb,),
        in_specs=[
            pl.BlockSpec((gb, m, w), lambda i: (i, 0, 0)),
            pl.BlockSpec((gb, batch, w), lambda i: (i, 0, 0)),
            pl.BlockSpec((None, gb, w, w), wsel),
            pl.BlockSpec((None, gb, w, w), wsel),
            pl.BlockSpec((None, gb, w, w), wsel),
            pl.BlockSpec((None, gb, 1, w), wsel),
        ],
        out_specs=[pl.BlockSpec((gb, m, w), lambda i: (i, 0, 0)), pl.BlockSpec((gb, batch, w), lambda i: (i, 0, 0))],
        out_shape=[jax.ShapeDtypeStruct((g, m, w), BF16), jax.ShapeDtypeStruct((g, batch, w), F32)],
        scratch_shapes=[pltpu.VMEM((m, w), F32), pltpu.VMEM((m, w), F32)],
        compiler_params=_cparams("parallel"),
        name="ssm_chunk_scan",
    )(u_r, h0, t_m, p_m, r_m, a_m)


def _ssm_post_kernel(y_ref, w_ref, b_ref, o_ref):
    g = jax.nn.gelu(y_ref[...].astype(F32))
    s = jnp.dot(g.astype(BF16), w_ref[...], preferred_element_type=F32) + b_ref[...]
    o_ref[...] = (g * jax.nn.sigmoid(s)).astype(BF16)


def _ssm_post(y, w_glu, b_glu, layer, tm=512):
    t, w = y.shape
    tm = _tile(t, tm)
    lsel = lambda i: (layer, 0, 0)
    return pl.pallas_call(
        _ssm_post_kernel,
        grid=(t // tm,),
        in_specs=[pl.BlockSpec((tm, w), lambda i: (i, 0)), pl.BlockSpec((None, w, w), lsel),
                  pl.BlockSpec((None, 1, w), lsel)],
        out_specs=pl.BlockSpec((tm, w), lambda i: (i, 0)),
        out_shape=jax.ShapeDtypeStruct((t, w), BF16),
        compiler_params=_cparams("parallel"),
        name="ssm_gelu_glu",
    )(y, w_glu, b_glu)


def _merge_kernel(ao_ref, ca_ref, sa_ref, ga_ref, gc_ref, gs_ref, wa_ref, wc_ref, ws_ref, o_ref):
    a = jnp.dot(ao_ref[...], wa_ref[...], preferred_element_type=F32)
    c = jnp.dot(ca_ref[...], wc_ref[...], preferred_element_type=F32)
    s = jnp.dot(sa_ref[...], ws_ref[...], preferred_element_type=F32)
    mix = ga_ref[...].astype(F32) * a + gc_ref[...].astype(F32) * c + gs_ref[...].astype(F32) * s
    o_ref[...] = mix.astype(BF16)


def _merge(ao, ca, sa, gates, w_attn_o, w_conv_o, w_ssm_o, layer, tm=512, tn=512):
    t, d = ao.shape[0], w_attn_o.shape[2]
    tm, tn = _tile(t, tm), _tile(d, tn)
    nj = d // tn
    act = lambda a: pl.BlockSpec((tm, a.shape[1]), lambda i, j: (i, 0))
    wsp = lambda w: pl.BlockSpec((None, w.shape[1], tn), lambda i, j: (layer, 0, j))
    gate = lambda k: pl.BlockSpec((tm, tn), lambda i, j: (i, k * nj + j))
    return pl.pallas_call(
        _merge_kernel,
        grid=(t // tm, nj),
        in_specs=[act(ao), act(ca), act(sa), gate(0), gate(1), gate(2),
                  wsp(w_attn_o), wsp(w_conv_o), wsp(w_ssm_o)],
        out_specs=pl.BlockSpec((tm, tn), lambda i, j: (i, j)),
        out_shape=jax.ShapeDtypeStruct((t, d), BF16),
        compiler_params=_cparams("parallel", "parallel"),
        name="branch_merge",
    )(ao, ca, sa, gates, gates, gates, w_attn_o, w_conv_o, w_ssm_o)


def _mmres_kernel(a_ref, w_ref, g_ref, gate_ref, x_ref, o_ref, acc_ref, *, nk):
    k = pl.program_id(1)
    part = jnp.dot(a_ref[...], w_ref[...], preferred_element_type=F32)

    def finish(acc):
        o_ref[...] = x_ref[...] + gate_ref[0] * _rms(acc, g_ref[...])

    if nk == 1:
        finish(part)
    else:
        @pl.when(k == 0)
        def _():
            acc_ref[...] = part

        @pl.when(jnp.logical_and(k > 0, k < nk - 1))
        def _():
            acc_ref[...] += part

        @pl.when(k == nk - 1)
        def _():
            finish(acc_ref[...] + part)


def _mmres(a, w, g_post, gate, x, layer, seq_len, tm=512, tk=1024):
    t, kd = a.shape
    d = w.shape[2]
    tm, tk = _tile(seq_len, tm), _tile(kd, tk)
    nk = kd // tk
    per_seq = seq_len // tm
    nb = gate.shape[0]
    bidx = (lambda i: i // per_seq) if nb > 1 else (lambda i: 0)
    return pl.pallas_call(
        functools.partial(_mmres_kernel, nk=nk),
        grid=(t // tm, nk),
        in_specs=[
            pl.BlockSpec((tm, tk), lambda i, k: (i, k)),
            pl.BlockSpec((None, tk, d), lambda i, k: (layer, k, 0)),
            pl.BlockSpec((None, 1, d), lambda i, k: (layer, 0, 0)),
            pl.BlockSpec((1, 1, d), lambda i, k: (bidx(i), 0, 0)),
            pl.BlockSpec((tm, d), lambda i, k: (i, 0)),
        ],
        out_specs=pl.BlockSpec((tm, d), lambda i, k: (i, 0)),
        out_shape=jax.ShapeDtypeStruct((t, d), F32),
        scratch_shapes=[pltpu.VMEM((tm, d), F32)],
        compiler_params=_cparams("parallel", "arbitrary"),
        name="mm_norm_residual",
    )(a, w, g_post, gate, x)


def _rope_swap_index():
    q = ROPE_DIM // 4
    return jnp.concatenate([jnp.arange(q, 2 * q), jnp.arange(0, q), jnp.arange(3 * q, 4 * q), jnp.arange(2 * q, 3 * q)])


def _rope_tables(n_tok):
    pairs = ROPE_DIM // 4
    pos = jnp.arange(n_tok)
    row = (pos // GRID_W).astype(F32)
    col = (pos % GRID_W).astype(F32)
    inv = ROPE_BASE ** (-jnp.arange(pairs, dtype=F32) / pairs)
    ang = jnp.stack([row[:, None] * inv, col[:, None] * inv], axis=1)
    cos, sin = jnp.cos(ang), jnp.sin(ang)
    cc = jnp.stack([cos, cos], axis=2).reshape(n_tok, ROPE_DIM)
    ss = jnp.stack([-sin, sin], axis=2).reshape(n_tok, ROPE_DIM)
    z = jnp.zeros((n_tok, ROPE_DIM), F32)
    return jnp.concatenate([cc, z], axis=1), jnp.concatenate([ss, z], axis=1)


def _identity_tables(n_tok):
    one = jnp.ones((n_tok, ROPE_DIM), F32)
    z = jnp.zeros((n_tok, ROPE_DIM), F32)
    return jnp.concatenate([one, z], axis=1), jnp.zeros((n_tok, 2 * ROPE_DIM), F32)


def _ssm_matrices(a_re, a_im, log_dt, b_re, b_im, c_re, c_im, d_skip):
    hp = lax.Precision.HIGHEST
    q, gc = SSM_CHUNK, SSM_GROUP_CH
    ly, _, g, p = a_re.shape
    dt = jnp.exp(log_dt)[..., None]
    tau = jnp.arange(q + 1, dtype=F32)[:, None]
    mag = jnp.exp((a_re * dt)[:, :, :, None, :] * tau)
    ang = (a_im * dt)[:, :, :, None, :] * tau
    pw_re, pw_im = mag * jnp.cos(ang), mag * jnp.sin(ang)
    lb_re, lb_im = pw_re[:, :, :, 1], pw_im[:, :, :, 1]
    den = a_re * a_re + a_im * a_im
    nr = lb_re - 1.0
    f_re = (nr * a_re + lb_im * a_im) / den
    f_im = (lb_im * a_re - nr * a_im) / den
    bb_re = f_re[..., None] * b_re - f_im[..., None] * b_im
    bb_im = f_re[..., None] * b_im + f_im[..., None] * b_re
    cl_re = c_re[:, :, :, None] * pw_re[:, :, :, :, None, :] - c_im[:, :, :, None] * pw_im[:, :, :, :, None, :]
    cl_im = c_re[:, :, :, None] * pw_im[:, :, :, :, None, :] + c_im[:, :, :, None] * pw_re[:, :, :, :, None, :]
    kern = (jnp.einsum("ldgtcp,ldgpe->ldgtce", cl_re, bb_re, precision=hp)
            - jnp.einsum("ldgtcp,ldgpe->ldgtce", cl_im, bb_im, precision=hp))
    s_i = jnp.arange(q)[:, None]
    t_i = jnp.arange(q)[None, :]
    diff = t_i - s_i
    k_f = jnp.take(kern[:, 0], jnp.clip(diff, 0, q - 1), axis=2)
    k_b = jnp.take(kern[:, 1], jnp.clip(-diff, 0, q - 1), axis=2)
    m4 = lambda cond: cond[None, None, :, :, None, None]
    skip = jnp.eye(gc, dtype=F32) * d_skip.reshape(ly, g, 1, 1, gc, 1)
    t_m = (jnp.where(m4(diff >= 0), k_f, 0.0) + jnp.where(m4(diff <= 0), k_b, 0.0)
           + jnp.where(m4(diff == 0), skip, 0.0))
    t_m = t_m.transpose(0, 1, 2, 5, 3, 4).reshape(ly, g, q * gc, q * gc)

    def inj(d, powers):
        pr = jnp.take(pw_re[:, d], powers, axis=2)
        pi = jnp.take(pw_im[:, d], powers, axis=2)
        br, bi = bb_re[:, d], bb_im[:, d]
        re = pr[..., None] * br[:, :, None] - pi[..., None] * bi[:, :, None]
        im = pr[..., None] * bi[:, :, None] + pi[..., None] * br[:, :, None]
        to_rows = lambda x: x.transpose(0, 1, 2, 4, 3).reshape(ly, g, q * gc, p)
        return to_rows(re), to_rows(im)

    pf_re, pf_im = inj(0, q - 1 - jnp.arange(q))
    pb_re, pb_im = inj(1, jnp.arange(q))
    p_m = jnp.concatenate([pf_re, pb_re, pf_im, pb_im], axis=-1)

    def carry(d, powers):
        re = jnp.take(cl_re[:, d], powers, axis=2)
        im = jnp.take(cl_im[:, d], powers, axis=2)
        to_cols = lambda x: x.transpose(0, 1, 4, 2, 3).reshape(ly, g, p, q * gc)
        return to_cols(re), to_cols(-im)

    rf_re, rf_im = carry(0, jnp.arange(q) + 1)
    rb_re, rb_im = carry(1, q - jnp.arange(q))
    r_m = jnp.concatenate([rf_re, rb_re, rf_im, rb_im], axis=2)
    a_m = jnp.concatenate([pw_re[:, 0, :, q], pw_re[:, 1, :, q], pw_im[:, 0, :, q], pw_im[:, 1, :, q]], axis=-1)
    return t_m.astype(BF16), p_m.astype(BF16), r_m.astype(BF16), a_m[:, :, None, :]


def _prepare_params(w_mod, b_mod, w_in, w_uq, w_ukv, w_attn_o, w_conv_o, w_glu, w_ssm_o, w_out, w_ff1, w_ff2,
                    q_lora, kv_lora, conv_w, ssm_w):
    ly = w_in.shape[0]
    swap = _rope_swap_index()
    o = [0, q_lora, q_lora + kv_lora, q_lora + kv_lora + ROPE_DIM]
    o += [o[3] + conv_w, o[3] + 2 * conv_w, o[3] + 2 * conv_w + ssm_w]
    wi = w_in.astype(BF16)
    kpe = wi[:, :, o[2]:o[3]]
    qk = NOPE_DIM + ROPE_DIM
    uq = w_uq.astype(BF16).reshape(ly, q_lora, N_HEADS, qk)
    pe = uq[..., NOPE_DIM:]
    ukv = w_ukv.astype(BF16).reshape(ly, kv_lora, N_HEADS, NOPE_DIM + V_DIM)
    return dict(
        w_mod=w_mod.astype(BF16), b_mod=b_mod[:, None, :],
        w_a=jnp.concatenate([wi[:, :, :o[2]], kpe, kpe[:, :, swap]], axis=2),
        w_conv_a=wi[:, :, o[3]:o[4]], w_conv_b=wi[:, :, o[4]:o[5]],
        w_ssm=wi[:, :, o[5]:o[6]], w_gate=wi[:, :, o[6]:],
        w_uq=jnp.concatenate([uq[..., :NOPE_DIM], pe, pe[..., swap]], axis=-1).reshape(ly, q_lora, N_HEADS * HEAD_W),
        w_uk=ukv[..., :NOPE_DIM].reshape(ly, kv_lora, N_HEADS * NOPE_DIM),
        w_uv=ukv[..., NOPE_DIM:].reshape(ly, kv_lora, N_HEADS * V_DIM),
        w_attn_o=w_attn_o.astype(BF16), w_conv_o=w_conv_o.astype(BF16), w_glu=w_glu.astype(BF16),
        w_ssm_o=w_ssm_o.astype(BF16), w_out=w_out.astype(BF16), w_ff1=w_ff1.astype(BF16),
        w_ff2=w_ff2.astype(BF16),
    )


def _block(x, mod, pp, vec, ssm_m, layer, batch, seq_len, rope_tabs, cache):
    shift1, scale1, gate1, shift2, scale2, gate2 = mod
    t = x.shape[0]
    cc, ss = rope_tabs
    norm_in = (x, vec["g_mix_pre"], scale1, shift1)

    q, ckv, kpe2 = _attn_in(*norm_in, pp["w_a"], vec["g_q"], vec["g_kv"], pp["w_uq"], cc, ss, layer, seq_len)
    k_new, v_new = _kv_build(ckv, kpe2, cc, ss, pp["w_uk"], pp["w_uv"], layer)
    shape3 = lambda a, b: a.reshape(b, a.shape[0] // b, a.shape[1])
    kvs = [(shape3(k_new, batch), shape3(v_new, batch))]
    if cache is not None:
        ckv_ctx, kpe_ctx, icc, iss = cache
        k_ctx, v_ctx = _kv_build(ckv_ctx, kpe_ctx, icc, iss, pp["w_uk"], pp["w_uv"], layer)
        kvs.append((shape3(k_ctx, batch), shape3(v_ctx, batch)))
    heads_per_step = min(N_HEADS, 8 if seq_len <= 256 else 4)
    ao = _attention(shape3(q, batch), kvs, heads_per_step).reshape(t, N_HEADS * V_DIM)

    z = _normmm(*norm_in, [pp["w_conv_a"], pp["w_conv_b"]], layer, seq_len, "glu")
    ca = _conv_branch(z, vec["conv_w"], vec["conv_b"], vec["conv_ln_g"], vec["conv_ln_b"], layer, batch, seq_len)

    u = _normmm(*norm_in, [pp["w_ssm"]], layer, seq_len, "none")
    g = u.shape[1] // SSM_GROUP_CH
    nc = seq_len // SSM_CHUNK
    u_r = (u.reshape(batch, nc, SSM_CHUNK, g, SSM_GROUP_CH).transpose(3, 1, 0, 2, 4)
           .reshape(g, nc * batch, SSM_CHUNK * SSM_GROUP_CH))
    y_r, fin = _ssm_scan(u_r, ssm_m["h0"], ssm_m["t"], ssm_m["p"], ssm_m["r"], ssm_m["a"], layer, batch)
    y = (y_r.reshape(g, nc, batch, SSM_CHUNK, SSM_GROUP_CH).transpose(2, 1, 3, 0, 4).reshape(t, u.shape[1]))
    sa = _ssm_post(y, pp["w_glu"], vec["b_glu"], layer)

    gates = _normmm(*norm_in, [pp["w_gate"]], layer, seq_len, "sigmoid")
    mix = _merge(ao, ca, sa, gates, pp["w_attn_o"], pp["w_conv_o"], pp["w_ssm_o"], layer)
    x = _mmres(mix, pp["w_out"], vec["g_mix_post"], gate1, x, layer, seq_len)

    f = _normmm(x, vec["g_mlp_pre"], scale2, shift2, [pp["w_ff1"]], layer, seq_len, "relu2")
    x = _mmres(f, pp["w_ff2"], vec["g_mlp_post"], gate2, x, layer, seq_len)
    return x, ckv, kpe2[:, :ROPE_DIM], fin


def kernel(x_prompt, x_sample, cache_ckv, cache_kpe, state_ssm_re, state_ssm_im, c, c_ctx, w_mod, b_mod, g_mix_pre, g_mix_post, g_mlp_pre, g_mlp_post, w_in, g_q, w_uq, g_kv, w_ukv, w_attn_o, conv_w, conv_b, conv_ln_g, conv_ln_b, w_conv_o, ssm_a_re, ssm_a_im, ssm_log_dt, ssm_b_re, ssm_b_im, ssm_c_re, ssm_c_im, ssm_d, w_glu, b_glu, w_ssm_o, w_out, w_ff1, w_ff2):
    bp, lp, d = x_prompt.shape
    bs, ls, _ = x_sample.shape
    depth = w_in.shape[0]
    past = cache_ckv.shape[2]
    q_lora, kv_lora = g_q.shape[1], g_kv.shape[1]
    conv_ch, ssm_ch = conv_w.shape[2], ssm_d.shape[1]
    n_groups = ssm_ch // SSM_GROUP_CH

    pp = _prepare_params(w_mod, b_mod, w_in, w_uq, w_ukv, w_attn_o, w_conv_o, w_glu, w_ssm_o, w_out, w_ff1,
                         w_ff2, q_lora, kv_lora, conv_ch, ssm_ch)
    row = lambda a: a[:, None, :]
    vec = dict(g_mix_pre=row(g_mix_pre), g_mix_post=row(g_mix_post), g_mlp_pre=row(g_mlp_pre),
               g_mlp_post=row(g_mlp_post), g_q=row(g_q), g_kv=row(g_kv), conv_w=conv_w, conv_b=row(conv_b),
               conv_ln_g=row(conv_ln_g), conv_ln_b=row(conv_ln_b), b_glu=row(b_glu))
    t_m, p_m, r_m, a_m = _ssm_matrices(ssm_a_re, ssm_a_im, ssm_log_dt, ssm_b_re, ssm_b_im, ssm_c_re, ssm_c_im,
                                       ssm_d)

    n_cond = 1 + bs
    cond = jnp.concatenate([c_ctx[None, :], c, jnp.zeros((-n_cond % 8, d), F32)], axis=0)
    mod_all = _modulation(cond, pp["w_mod"], pp["b_mod"])

    rope_s = _rope_tables(ls)
    ident_p = _identity_tables(min(lp, 256))
    ident_c = _identity_tables(min(past, 256))
    zero_h0 = jnp.zeros((n_groups, bp, 4 * SSM_STATE), F32)

    xp = x_prompt.reshape(bp * lp, d)
    xs = x_sample.reshape(bs * ls, d)
    ckvs, kpes, fins = [], [], []
    for l in range(depth):
        mods = jnp.split(mod_all[l], 6, axis=-1)
        mod_ctx = [m[0:1, None, :] for m in mods]
        mod_lat = [m[1:n_cond, None, :] for m in mods]
        ssm_p = dict(t=t_m, p=p_m, r=r_m, a=a_m, h0=zero_h0)
        xp, ckv, kpe, fin = _block(xp, mod_ctx, pp, vec, ssm_p, l, bp, lp, ident_p, None)
        ckvs.append(ckv.reshape(bp, lp, kv_lora))
        kpes.append(kpe.reshape(bp, lp, ROPE_DIM))
        fins.append(fin)
        sr, si = state_ssm_re[:, l], state_ssm_im[:, l]
        h0 = jnp.concatenate([sr[:, 0], sr[:, 1], si[:, 0], si[:, 1]], axis=-1).transpose(1, 0, 2)
        kpe_ctx = jnp.pad(cache_kpe[:, l].reshape(bs * past, ROPE_DIM), ((0, 0), (0, ROPE_DIM)))
        cache = (cache_ckv[:, l].reshape(bs * past, kv_lora), kpe_ctx) + ident_c
        ssm_s = dict(t=t_m, p=p_m, r=r_m, a=a_m, h0=h0)
        xs = _block(xs, mod_lat, pp, vec, ssm_s, l, bs, ls, rope_s, cache)[0]

    new_ckv = jnp.stack(ckvs, axis=1)
    new_kpe = jnp.stack(kpes, axis=1)
    fin = jnp.stack(fins, axis=0)
    fin = fin.reshape(depth, n_groups, bp, 2, 2, SSM_STATE).transpose(3, 2, 0, 4, 1, 5)
    return (xp.reshape(bp, lp, d), xs.reshape(bs, ls, d), new_ckv, new_kpe, fin[0], fin[1])
```

```python
import functools
import math

import jax
import jax.numpy as jnp
from jax import lax
from jax.experimental import pallas as pl
from jax.experimental.pallas import tpu as pltpu

N_HEADS = 16
NOPE_DIM = 128
ROPE_DIM = 64
V_DIM = 128
GRID_W = 64
ROPE_BASE = 10000.0
SSM_GROUP_CH = 16
SSM_STATE = 64
EPS = 1e-6

HEAD_W = NOPE_DIM + 2 * ROPE_DIM
SSM_CHUNK = 16
V7X_VMEM_LIMIT_BYTES = 56 * 1024 * 1024

F32 = jnp.float32
BF16 = jnp.bfloat16


def _cparams(*sem):
    return pltpu.CompilerParams(dimension_semantics=sem, vmem_limit_bytes=V7X_VMEM_LIMIT_BYTES)


def _tile(n, target):
    if n <= target:
        return n
    for t in range(target, 7, -1):
        if n % t == 0 and t % 8 == 0:
            return t
    return n


def _rms(x, g):
    return x * lax.rsqrt(jnp.mean(x * x, axis=-1, keepdims=True) + EPS) * g


def _rmsmod(x, g, scale, shift):
    return _rms(x, g) * (1.0 + scale) + shift


def _mod_kernel(c_ref, w_ref, b_ref, o_ref):
    h = jax.nn.silu(c_ref[...]).astype(BF16)
    o_ref[...] = jnp.dot(h, w_ref[...], preferred_element_type=F32) + b_ref[...]


def _modulation(cond, w_mod, b_mod):
    ly, d, n = w_mod.shape
    r = cond.shape[0]
    tn = _tile(n, 1536)
    return pl.pallas_call(
        _mod_kernel,
        grid=(ly, n // tn),
        in_specs=[
            pl.BlockSpec((r, d), lambda l, j: (0, 0)),
            pl.BlockSpec((None, d, tn), lambda l, j: (l, 0, j)),
            pl.BlockSpec((None, 1, tn), lambda l, j: (l, 0, j)),
        ],
        out_specs=pl.BlockSpec((None, r, tn), lambda l, j: (l, 0, j)),
        out_shape=jax.ShapeDtypeStruct((ly, r, n), F32),
        compiler_params=_cparams("parallel", "parallel"),
        name="modulation",
    )(cond, w_mod, b_mod)


def _normmm_kernel(x_ref, g_ref, sc_ref, sh_ref, *rest, n_w, epilogue):
    w_refs, o_ref, h_ref = rest[:n_w], rest[n_w], rest[n_w + 1]

    @pl.when(pl.program_id(1) == 0)
    def _():
        h_ref[...] = _rmsmod(x_ref[...], g_ref[...], sc_ref[0], sh_ref[0]).astype(BF16)

    h = h_ref[...]
    a = jnp.dot(h, w_refs[0][...], preferred_element_type=F32)
    if epilogue == "glu":
        a = a * jax.nn.sigmoid(jnp.dot(h, w_refs[1][...], preferred_element_type=F32))
    elif epilogue == "sigmoid":
        a = jax.nn.sigmoid(a)
    o_ref[...] = a.astype(o_ref.dtype)


def _normmm(x, g, scale, shift, ws, layer, seq_len, epilogue, out_dtype=BF16, tm=1024, tn=1024):
    t, d = x.shape
    n = ws[0].shape[2]
    nb = scale.shape[0]
    tm = _tile(t if nb == 1 else seq_len, tm)
    tn = _tile(n, tn // len(ws))
    per_seq = seq_len // tm
    bidx = (lambda i: i // per_seq) if nb > 1 else (lambda i: 0)
    w_spec = pl.BlockSpec((None, d, tn), lambda i, j: (layer, 0, j))
    return pl.pallas_call(
        functools.partial(_normmm_kernel, n_w=len(ws), epilogue=epilogue),
        grid=(t // tm, n // tn),
        in_specs=[
            pl.BlockSpec((tm, d), lambda i, j: (i, 0)),
            pl.BlockSpec((None, 1, d), lambda i, j: (layer, 0, 0)),
            pl.BlockSpec((1, 1, d), lambda i, j: (bidx(i), 0, 0)),
            pl.BlockSpec((1, 1, d), lambda i, j: (bidx(i), 0, 0)),
        ] + [w_spec] * len(ws),
        out_specs=pl.BlockSpec((tm, tn), lambda i, j: (i, j)),
        out_shape=jax.ShapeDtypeStruct((t, n), out_dtype),
        scratch_shapes=[pltpu.VMEM((tm, d), BF16)],
        compiler_params=_cparams("parallel", "arbitrary"),
        name="normmm_" + epilogue,
    )(x, g, scale, shift, *ws)


def _attn_in_kernel(x_ref, g_ref, sc_ref, sh_ref, wa_ref, gq_ref, gkv_ref, wuq_ref, cc_ref, ss_ref,
                    q_ref, ckv_ref, kpe_ref, *, q_lora, kv_lora):
    h = _rmsmod(x_ref[...], g_ref[...], sc_ref[0], sh_ref[0]).astype(BF16)
    a = jnp.dot(h, wa_ref[...], preferred_element_type=F32)
    cq = _rms(a[:, :q_lora], gq_ref[...]).astype(BF16)
    ckv_ref[...] = _rms(a[:, q_lora:q_lora + kv_lora], gkv_ref[...])
    kpe_ref[...] = a[:, q_lora + kv_lora:]
    q = jnp.dot(cq, wuq_ref[...], preferred_element_type=F32)
    cc, ss = cc_ref[...], ss_ref[...]
    for hd in range(N_HEADS):
        lo = hd * HEAD_W
        q_ref[:, lo:lo + NOPE_DIM] = q[:, lo:lo + NOPE_DIM].astype(BF16)
        v = q[:, lo + NOPE_DIM:lo + HEAD_W]
        q_ref[:, lo + NOPE_DIM:lo + HEAD_W] = (v * cc + pltpu.roll(v, ROPE_DIM, 1) * ss).astype(BF16)


def _attn_in(x, g, scale, shift, w_a, g_q, g_kv, w_uq, cc, ss, layer, seq_len, tm=256):
    t, d = x.shape
    na = w_a.shape[2]
    q_lora, kv_lora = g_q.shape[2], g_kv.shape[2]
    nq = w_uq.shape[2]
    tm = _tile(seq_len, tm)
    per_seq = seq_len // tm
    nb = scale.shape[0]
    bidx = (lambda i: i // per_seq) if nb > 1 else (lambda i: 0)
    ntab = cc.shape[0] // tm
    lsel = lambda i: (layer, 0, 0)
    return pl.pallas_call(
        functools.partial(_attn_in_kernel, q_lora=q_lora, kv_lora=kv_lora),
        grid=(t // tm,),
        in_specs=[
            pl.BlockSpec((tm, d), lambda i: (i, 0)),
            pl.BlockSpec((None, 1, d), lsel),
            pl.BlockSpec((1, 1, d), lambda i: (bidx(i), 0, 0)),
            pl.BlockSpec((1, 1, d), lambda i: (bidx(i), 0, 0)),
            pl.BlockSpec((None, d, na), lsel),
            pl.BlockSpec((None, 1, q_lora), lsel),
            pl.BlockSpec((None, 1, kv_lora), lsel),
            pl.BlockSpec((None, q_lora, nq), lsel),
            pl.BlockSpec((tm, 2 * ROPE_DIM), lambda i: (i % ntab, 0)),
            pl.BlockSpec((tm, 2 * ROPE_DIM), lambda i: (i % ntab, 0)),
        ],
        out_specs=[
            pl.BlockSpec((tm, nq), lambda i: (i, 0)),
            pl.BlockSpec((tm, kv_lora), lambda i: (i, 0)),
            pl.BlockSpec((tm, 2 * ROPE_DIM), lambda i: (i, 0)),
        ],
        out_shape=[
            jax.ShapeDtypeStruct((t, nq), BF16),
            jax.ShapeDtypeStruct((t, kv_lora), F32),
            jax.ShapeDtypeStruct((t, 2 * ROPE_DIM), F32),
        ],
        compiler_params=_cparams("parallel"),
        name="attn_in",
    )(x, g, scale, shift, w_a, g_q, g_kv, w_uq, cc, ss)


def _kv_build_kernel(ckv_ref, kpe_ref, cc_ref, ss_ref, wuk_ref, wuv_ref, k_ref, v_ref):
    c = ckv_ref[...].astype(BF16)
    kn = jnp.dot(c, wuk_ref[...], preferred_element_type=F32)
    v_ref[...] = jnp.dot(c, wuv_ref[...], preferred_element_type=F32).astype(BF16)
    kp = kpe_ref[...]
    kp = (kp * cc_ref[...] + pltpu.roll(kp, ROPE_DIM, 1) * ss_ref[...]).astype(BF16)
    for hd in range(N_HEADS):
        k_ref[:, hd * HEAD_W:hd * HEAD_W + NOPE_DIM] = kn[:, hd * NOPE_DIM:(hd + 1) * NOPE_DIM].astype(BF16)
        k_ref[:, hd * HEAD_W + NOPE_DIM:(hd + 1) * HEAD_W] = kp


def _kv_build(ckv, kpe2, cc, ss, w_uk, w_uv, layer, tm=256):
    t, kvl = ckv.shape
    tm = _tile(cc.shape[0], tm)
    ntab = cc.shape[0] // tm
    nk, nv = N_HEADS * HEAD_W, w_uv.shape[2]
    lsel = lambda i: (layer, 0, 0)
    return pl.pallas_call(
        _kv_build_kernel,
        grid=(t // tm,),
        in_specs=[
            pl.BlockSpec((tm, kvl), lambda i: (i, 0)),
            pl.BlockSpec((tm, 2 * ROPE_DIM), lambda i: (i, 0)),
            pl.BlockSpec((tm, 2 * ROPE_DIM), lambda i: (i % ntab, 0)),
            pl.BlockSpec((tm, 2 * ROPE_DIM), lambda i: (i % ntab, 0)),
            pl.BlockSpec((None, kvl, w_uk.shape[2]), lsel),
            pl.BlockSpec((None, kvl, nv), lsel),
        ],
        out_specs=[pl.BlockSpec((tm, nk), lambda i: (i, 0)), pl.BlockSpec((tm, nv), lambda i: (i, 0))],
        out_shape=[jax.ShapeDtypeStruct((t, nk), BF16), jax.ShapeDtypeStruct((t, nv), BF16)],
        compiler_params=_cparams("parallel"),
        name="kv_build",
    )(ckv, kpe2, cc, ss, w_uk, w_uv)


def _attn_kernel(q_ref, *rest, n_src, heads):
    kv_refs, o_ref = rest[:2 * n_src], rest[2 * n_src]
    scale = 1.0 / math.sqrt(NOPE_DIM + ROPE_DIM)
    for hd in range(heads):
        q = q_ref[:, hd * HEAD_W:(hd + 1) * HEAD_W]
        s = [lax.dot_general(q, kv_refs[2 * i][:, hd * HEAD_W:(hd + 1) * HEAD_W],
                             (((1,), (1,)), ((), ())), preferred_element_type=F32) * scale
             for i in range(n_src)]
        m = jnp.max(s[0], axis=-1, keepdims=True)
        for si in s[1:]:
            m = jnp.maximum(m, jnp.max(si, axis=-1, keepdims=True))
        acc, den = None, None
        for i in range(n_src):
            p = jnp.exp(s[i] - m)
            li = jnp.sum(p, axis=-1, keepdims=True)
            oi = jnp.dot(p.astype(BF16), kv_refs[2 * i + 1][:, hd * V_DIM:(hd + 1) * V_DIM],
                         preferred_element_type=F32)
            acc = oi if acc is None else acc + oi
            den = li if den is None else den + li
        o_ref[:, hd * V_DIM:(hd + 1) * V_DIM] = (acc / den).astype(BF16)


def _attention(q, kvs, heads_per_step, tq=256):
    b, sq, _ = q.shape
    tq = _tile(sq, tq)
    hb = heads_per_step
    in_specs = [pl.BlockSpec((None, tq, hb * HEAD_W), lambda bi, hi, qi: (bi, qi, hi))]
    args = [q]
    for k, v in kvs:
        sk = k.shape[1]
        in_specs.append(pl.BlockSpec((None, sk, hb * HEAD_W), lambda bi, hi, qi: (bi, 0, hi)))
        in_specs.append(pl.BlockSpec((None, sk, hb * V_DIM), lambda bi, hi, qi: (bi, 0, hi)))
        args += [k, v]
    return pl.pallas_call(
        functools.partial(_attn_kernel, n_src=len(kvs), heads=hb),
        grid=(b, N_HEADS // hb, sq // tq),
        in_specs=in_specs,
        out_specs=pl.BlockSpec((None, tq, hb * V_DIM), lambda bi, hi, qi: (bi, qi, hi)),
        out_shape=jax.ShapeDtypeStruct((b, sq, N_HEADS * V_DIM), BF16),
        compiler_params=_cparams("parallel", "parallel", "parallel"),
        name="attention",
    )(*args)


_CONV_ROWS = 32
_CONV_LANES = 256
_CONV_HALO = 16
_SUBLANES = 8


def _conv_kernel(z_ref, w_ref, b_ref, o_ref, zs_ref, wb_ref, *, seq_len, taps):
    lanes = z_ref.shape[-1]
    pad = (taps - 1) // 2
    halo = jnp.zeros((_CONV_HALO, lanes), F32)
    zp = jnp.concatenate([halo, z_ref[...].astype(F32), halo], axis=0)
    span = seq_len + 2 * _CONV_HALO - _SUBLANES
    for j in range(_SUBLANES):
        zs_ref[j, 0:span, :] = zp[j:j + span, :]
    for k in range(taps):
        wb_ref[k] = jnp.broadcast_to(w_ref[k:k + 1, :], (_SUBLANES, lanes))
    bias = jnp.broadcast_to(b_ref[...], (_SUBLANES, lanes))
    n_sub = _CONV_ROWS // _SUBLANES

    def conv_step(it, carry):
        r0 = pl.multiple_of(it * _CONV_ROWS, _CONV_ROWS)
        acc = [bias] * n_sub
        for k in range(taps):
            off = _CONV_HALO - pad + k
            j, base = off % _SUBLANES, off - off % _SUBLANES
            win = zs_ref[j, pl.ds(r0 + base, _CONV_ROWS), :]
            wk = wb_ref[k]
            acc = [acc[p] + wk * win[p * _SUBLANES:(p + 1) * _SUBLANES, :] for p in range(n_sub)]
        o_ref[pl.ds(r0, _CONV_ROWS), :] = jnp.concatenate(acc, axis=0).astype(BF16)
        return carry

    lax.fori_loop(0, seq_len // _CONV_ROWS, conv_step, 0)


def _conv_branch(z, conv_w, conv_b, layer, batch, seq_len):
    t, ch = z.shape
    taps = conv_w.shape[1]
    lanes = _tile(ch, _CONV_LANES)
    return pl.pallas_call(
        functools.partial(_conv_kernel, seq_len=seq_len, taps=taps),
        grid=(batch, ch // lanes),
        in_specs=[
            pl.BlockSpec((seq_len, lanes), lambda b, c: (b, c)),
            pl.BlockSpec((None, taps, lanes), lambda b, c: (layer, 0, c)),
            pl.BlockSpec((None, 1, lanes), lambda b, c: (layer, 0, c)),
        ],
        out_specs=pl.BlockSpec((seq_len, lanes), lambda b, c: (b, c)),
        out_shape=jax.ShapeDtypeStruct((t, ch), BF16),
        scratch_shapes=[pltpu.VMEM((_SUBLANES, seq_len + 2 * _CONV_HALO, lanes), F32),
                        pltpu.VMEM((taps, _SUBLANES, lanes), F32)],
        compiler_params=_cparams("parallel", "parallel"),
        name="depthwise_conv",
    )(z, conv_w, conv_b)


_LANES = 128
_SSM_TILE_GROUPS = _LANES // SSM_GROUP_CH
_SSM_MOVE_ROWS = 64


def _ssm_kernel(x_ref, h0_ref, t_ref, p_ref, rf_ref, rb_ref, a_ref, y_ref, fin_ref,
                u_scr, y_scr, ds_scr, sf_scr, sb_scr, *, n_chunks, batch):
    m = n_chunks * batch
    ng, q, gc = _SSM_TILE_GROUPS, SSM_CHUNK, SSM_GROUP_CH
    half = 2 * SSM_STATE
    rr = min(_SSM_MOVE_ROWS, m)
    blk = lax.broadcasted_iota(jnp.int32, (rr, _LANES), 1) // gc
    is_fwd = lax.broadcasted_iota(jnp.int32, (batch, half), 1) < SSM_STATE

    def blocks_to(src_block, srcs):
        acc = None
        for j in range(ng):
            shift = ((j - src_block) % ng) * gc
            piece = pltpu.roll(srcs[j], shift, 1) if shift else srcs[j]
            acc = piece if acc is None else jnp.where(blk == j, piece, acc)
        return acc

    def gather_in(it, carry):
        r0 = pl.multiple_of(it * rr, rr)
        xs = [x_ref[pl.ds(r0, rr), s, :] for s in range(q)]
        for g in range(ng):
            halves = [blocks_to(g, xs[h * ng:(h + 1) * ng]) for h in range(q // ng)]
            u_scr[g, pl.ds(r0, rr), :] = jnp.concatenate(halves, axis=1).astype(BF16)
        return carry

    lax.fori_loop(0, m // rr, gather_in, 0)

    def one_group(g, carry):
        u = u_scr[g]
        ds_scr[...] = jnp.dot(u, p_ref[g], preferred_element_type=F32).reshape(batch, n_chunks, 2 * half)
        a = a_ref[g]
        h0 = h0_ref[g]
        a_re, a_im = a[:, :half], a[:, half:]
        s_re, s_im = h0[:, :half], h0[:, half:]
        for c in range(n_chunks):
            cb = n_chunks - 1 - c
            sf_scr[:, c, :half] = s_re
            sf_scr[:, c, half:] = s_im
            sb_scr[:, cb, :half] = s_re
            sb_scr[:, cb, half:] = s_im
            d_re = jnp.where(is_fwd, ds_scr[:, c, :half], ds_scr[:, cb, :half])
            d_im = jnp.where(is_fwd, ds_scr[:, c, half:], ds_scr[:, cb, half:])
            s_re, s_im = (a_re * s_re - a_im * s_im + d_re, a_re * s_im + a_im * s_re + d_im)
        fin_ref[g] = jnp.concatenate([s_re, s_im], axis=1)
        y = jnp.dot(u, t_ref[g], preferred_element_type=F32)
        y = y + jnp.dot(sf_scr[...].reshape(m, 2 * half).astype(BF16), rf_ref[g], preferred_element_type=F32)
        y = y + jnp.dot(sb_scr[...].reshape(m, 2 * half).astype(BF16), rb_ref[g], preferred_element_type=F32)
        y_scr[g] = y
        return carry

    lax.fori_loop(0, ng, one_group, 0)

    def scatter_out(it, carry):
        r0 = pl.multiple_of(it * rr, rr)
        ys = [y_scr[g, pl.ds(r0, rr), :] for g in range(ng)]
        for t in range(q):
            h, j = divmod(t, ng)
            srcs = [y[:, h * _LANES:(h + 1) * _LANES] for y in ys]
            acc = None
            for g in range(ng):
                shift = ((g - j) % ng) * gc
                piece = pltpu.roll(srcs[g], shift, 1) if shift else srcs[g]
                acc = piece if acc is None else jnp.where(blk == g, piece, acc)
            y_ref[pl.ds(r0, rr), t, :] = acc
        return carry

    lax.fori_loop(0, m // rr, scatter_out, 0)


def _ssm_scan(u, h0, t_m, p_m, rf_m, rb_m, a_m, layer, batch):
    t, ch = u.shape
    m = t // SSM_CHUNK
    n_chunks = m // batch
    ng, w = _SSM_TILE_GROUPS, SSM_CHUNK * SSM_GROUP_CH
    g_all = ch // SSM_GROUP_CH
    tok = pl.BlockSpec((m, SSM_CHUNK, _LANES), lambda i: (0, 0, i))
    wsel = lambda i: (layer, i, 0, 0)
    mat = pl.BlockSpec((None, ng, w, w), wsel)
    st = pl.BlockSpec((ng, batch, w), lambda i: (i, 0, 0))
    y, fin = pl.pallas_call(
        functools.partial(_ssm_kernel, n_chunks=n_chunks, batch=batch),
        grid=(g_all // ng,),
        in_specs=[tok, st, mat, mat, mat, mat, pl.BlockSpec((None, ng, 1, w), wsel)],
        out_specs=[tok, st],
        out_shape=[jax.ShapeDtypeStruct((m, SSM_CHUNK, ch), F32), jax.ShapeDtypeStruct((g_all, batch, w), F32)],
        scratch_shapes=[pltpu.VMEM((ng, m, w), BF16), pltpu.VMEM((ng, m, w), F32),
                        pltpu.VMEM((batch, n_chunks, w), F32), pltpu.VMEM((batch, n_chunks, w), F32),
                        pltpu.VMEM((batch, n_chunks, w), F32)],
        compiler_params=_cparams("parallel"),
        name="ssm_chunk_scan",
    )(u.reshape(m, SSM_CHUNK, ch), h0, t_m, p_m, rf_m, rb_m, a_m)
    return y.reshape(t, ch), fin


def _ssm_post_kernel(y_ref, w_ref, b_ref, o_ref):
    g = jax.nn.gelu(y_ref[...].astype(F32))
    s = jnp.dot(g.astype(BF16), w_ref[...], preferred_element_type=F32) + b_ref[...]
    o_ref[...] = (g * jax.nn.sigmoid(s)).astype(BF16)


def _ssm_post(y, w_glu, b_glu, layer, tm=512):
    t, w = y.shape
    tm = _tile(t, tm)
    lsel = lambda i: (layer, 0, 0)
    return pl.pallas_call(
        _ssm_post_kernel,
        grid=(t // tm,),
        in_specs=[pl.BlockSpec((tm, w), lambda i: (i, 0)), pl.BlockSpec((None, w, w), lsel),
                  pl.BlockSpec((None, 1, w), lsel)],
        out_specs=pl.BlockSpec((tm, w), lambda i: (i, 0)),
        out_shape=jax.ShapeDtypeStruct((t, w), BF16),
        compiler_params=_cparams("parallel"),
        name="ssm_gelu_glu",
    )(y, w_glu, b_glu)


def _accumulate_then(acc_ref, part, k, nk, finish):
    if nk == 1:
        finish(part)
        return

    @pl.when(k == 0)
    def _():
        acc_ref[...] = part

    @pl.when(jnp.logical_and(k > 0, k < nk - 1))
    def _():
        acc_ref[...] += part

    @pl.when(k == nk - 1)
    def _():
        finish(acc_ref[...] + part)


def _branch_out_kernel(ao_ref, cv_ref, sa_ref, ga_ref, gc_ref, gs_ref, wa_ref, wc_ref, ws_ref, wo_ref,
                       lg_ref, lb_ref, g_ref, gate_ref, x_ref, o_ref, ca_ref, acc_ref, *, nj):
    j = pl.program_id(1)

    @pl.when(j == 0)
    def _():
        v = cv_ref[...].astype(F32)
        vc = v - jnp.mean(v, axis=-1, keepdims=True)
        var = jnp.mean(vc * vc, axis=-1, keepdims=True)
        y = vc * lax.rsqrt(var + EPS) * lg_ref[...] + lb_ref[...]
        ca_ref[...] = jax.nn.silu(y).astype(BF16)

    a = jnp.dot(ao_ref[...], wa_ref[...], preferred_element_type=F32)
    c = jnp.dot(ca_ref[...], wc_ref[...], preferred_element_type=F32)
    s = jnp.dot(sa_ref[...], ws_ref[...], preferred_element_type=F32)
    mix = ga_ref[...].astype(F32) * a + gc_ref[...].astype(F32) * c + gs_ref[...].astype(F32) * s
    part = jnp.dot(mix.astype(BF16), wo_ref[...], preferred_element_type=F32)

    def finish(total):
        o_ref[...] = x_ref[...] + gate_ref[0] * _rms(total, g_ref[...])

    _accumulate_then(acc_ref, part, j, nj, finish)


def _branch_out(ao, cv, sa, gates, x, w_attn_o, w_conv_o, w_ssm_o, w_out, ln_g, ln_b, g_post, gate, layer,
                seq_len, tm=512, tn=512):
    t, d = x.shape
    nb = gate.shape[0]
    tm, tn = _tile(t if nb == 1 else seq_len, tm), _tile(d, tn)
    nj = d // tn
    per_seq = seq_len // tm
    bidx = (lambda i: i // per_seq) if nb > 1 else (lambda i: 0)
    act = lambda a: pl.BlockSpec((tm, a.shape[1]), lambda i, j: (i, 0))
    wcol = lambda w: pl.BlockSpec((None, w.shape[1], tn), lambda i, j: (layer, 0, j))
    gcol = lambda k: pl.BlockSpec((tm, tn), lambda i, j: (i, k * nj + j))
    vec = lambda v: pl.BlockSpec((None, 1, v.shape[2]), lambda i, j: (layer, 0, 0))
    return pl.pallas_call(
        functools.partial(_branch_out_kernel, nj=nj),
        grid=(t // tm, nj),
        in_specs=[act(ao), act(cv), act(sa), gcol(0), gcol(1), gcol(2),
                  wcol(w_attn_o), wcol(w_conv_o), wcol(w_ssm_o),
                  pl.BlockSpec((None, tn, d), lambda i, j: (layer, j, 0)),
                  vec(ln_g), vec(ln_b), vec(g_post),
                  pl.BlockSpec((1, 1, d), lambda i, j: (bidx(i), 0, 0)),
                  pl.BlockSpec((tm, d), lambda i, j: (i, 0))],
        out_specs=pl.BlockSpec((tm, d), lambda i, j: (i, 0)),
        out_shape=jax.ShapeDtypeStruct((t, d), F32),
        scratch_shapes=[pltpu.VMEM((tm, cv.shape[1]), BF16), pltpu.VMEM((tm, d), F32)],
        compiler_params=_cparams("parallel", "arbitrary"),
        name="branch_out",
    )(ao, cv, sa, gates, gates, gates, w_attn_o, w_conv_o, w_ssm_o, w_out, ln_g, ln_b, g_post, gate, x)


def _mlp_kernel(x_ref, gpre_ref, sc_ref, sh_ref, w1_ref, w2_ref, gpost_ref, gate_ref, o_ref, h_ref, acc_ref, *,
                nk):
    k = pl.program_id(1)

    @pl.when(k == 0)
    def _():
        h_ref[...] = _rmsmod(x_ref[...], gpre_ref[...], sc_ref[0], sh_ref[0]).astype(BF16)

    f = jnp.square(jnp.maximum(jnp.dot(h_ref[...], w1_ref[...], preferred_element_type=F32), 0.0))
    part = jnp.dot(f.astype(BF16), w2_ref[...], preferred_element_type=F32)

    def finish(total):
        o_ref[...] = x_ref[...] + gate_ref[0] * _rms(total, gpost_ref[...])

    _accumulate_then(acc_ref, part, k, nk, finish)


def _mlp(x, g_pre, scale, shift, w1, w2, g_post, gate, layer, seq_len, tm=512, tf=1024):
    t, d = x.shape
    ff = w1.shape[2]
    nb = gate.shape[0]
    tm, tf = _tile(t if nb == 1 else seq_len, tm), _tile(ff, tf)
    nk = ff // tf
    per_seq = seq_len // tm
    bidx = (lambda i: i // per_seq) if nb > 1 else (lambda i: 0)
    vec = pl.BlockSpec((None, 1, d), lambda i, k: (layer, 0, 0))
    mod = pl.BlockSpec((1, 1, d), lambda i, k: (bidx(i), 0, 0))
    return pl.pallas_call(
        functools.partial(_mlp_kernel, nk=nk),
        grid=(t // tm, nk),
        in_specs=[pl.BlockSpec((tm, d), lambda i, k: (i, 0)), vec, mod, mod,
                  pl.BlockSpec((None, d, tf), lambda i, k: (layer, 0, k)),
                  pl.BlockSpec((None, tf, d), lambda i, k: (layer, k, 0)),
                  vec, mod],
        out_specs=pl.BlockSpec((tm, d), lambda i, k: (i, 0)),
        out_shape=jax.ShapeDtypeStruct((t, d), F32),
        scratch_shapes=[pltpu.VMEM((tm, d), BF16), pltpu.VMEM((tm, d), F32)],
        compiler_params=_cparams("parallel", "arbitrary"),
        name="mlp_relu2",
    )(x, g_pre, scale, shift, w1, w2, g_post, gate)


def _rope_swap_index():
    q = ROPE_DIM // 4
    return jnp.concatenate([jnp.arange(q, 2 * q), jnp.arange(0, q), jnp.arange(3 * q, 4 * q), jnp.arange(2 * q, 3 * q)])


def _rope_tables(n_tok):
    pairs = ROPE_DIM // 4
    pos = jnp.arange(n_tok)
    row = (pos // GRID_W).astype(F32)
    col = (pos % GRID_W).astype(F32)
    inv = ROPE_BASE ** (-jnp.arange(pairs, dtype=F32) / pairs)
    ang = jnp.stack([row[:, None] * inv, col[:, None] * inv], axis=1)
    cos, sin = jnp.cos(ang), jnp.sin(ang)
    cc = jnp.stack([cos, cos], axis=2).reshape(n_tok, ROPE_DIM)
    ss = jnp.stack([-sin, sin], axis=2).reshape(n_tok, ROPE_DIM)
    z = jnp.zeros((n_tok, ROPE_DIM), F32)
    return jnp.concatenate([cc, z], axis=1), jnp.concatenate([ss, z], axis=1)


def _identity_tables(n_tok):
    one = jnp.ones((n_tok, ROPE_DIM), F32)
    z = jnp.zeros((n_tok, ROPE_DIM), F32)
    return jnp.concatenate([one, z], axis=1), jnp.zeros((n_tok, 2 * ROPE_DIM), F32)


def _ssm_matrices(a_re, a_im, log_dt, b_re, b_im, c_re, c_im, d_skip):
    hp = lax.Precision.HIGHEST
    q, gc = SSM_CHUNK, SSM_GROUP_CH
    ly, _, g, p = a_re.shape
    dt = jnp.exp(log_dt)[..., None]
    tau = jnp.arange(q + 1, dtype=F32)[:, None]
    mag = jnp.exp((a_re * dt)[:, :, :, None, :] * tau)
    ang = (a_im * dt)[:, :, :, None, :] * tau
    pw_re, pw_im = mag * jnp.cos(ang), mag * jnp.sin(ang)
    lb_re, lb_im = pw_re[:, :, :, 1], pw_im[:, :, :, 1]
    den = a_re * a_re + a_im * a_im
    nr = lb_re - 1.0
    f_re = (nr * a_re + lb_im * a_im) / den
    f_im = (lb_im * a_re - nr * a_im) / den
    bb_re = f_re[..., None] * b_re - f_im[..., None] * b_im
    bb_im = f_re[..., None] * b_im + f_im[..., None] * b_re
    cl_re = c_re[:, :, :, None] * pw_re[:, :, :, :, None, :] - c_im[:, :, :, None] * pw_im[:, :, :, :, None, :]
    cl_im = c_re[:, :, :, None] * pw_im[:, :, :, :, None, :] + c_im[:, :, :, None] * pw_re[:, :, :, :, None, :]
    kern = (jnp.einsum("ldgtcp,ldgpe->ldgtce", cl_re, bb_re, precision=hp)
            - jnp.einsum("ldgtcp,ldgpe->ldgtce", cl_im, bb_im, precision=hp))
    s_i = jnp.arange(q)[:, None]
    t_i = jnp.arange(q)[None, :]
    diff = t_i - s_i
    k_f = jnp.take(kern[:, 0], jnp.clip(diff, 0, q - 1), axis=2)
    k_b = jnp.take(kern[:, 1], jnp.clip(-diff, 0, q - 1), axis=2)
    m4 = lambda cond: cond[None, None, :, :, None, None]
    skip = jnp.eye(gc, dtype=F32) * d_skip.reshape(ly, g, 1, 1, gc, 1)
    t_m = (jnp.where(m4(diff >= 0), k_f, 0.0) + jnp.where(m4(diff <= 0), k_b, 0.0)
           + jnp.where(m4(diff == 0), skip, 0.0))
    t_m = t_m.transpose(0, 1, 2, 5, 3, 4).reshape(ly, g, q * gc, q * gc)

    def inj(d, powers):
        pr = jnp.take(pw_re[:, d], powers, axis=2)
        pi = jnp.take(pw_im[:, d], powers, axis=2)
        br, bi = bb_re[:, d], bb_im[:, d]
        re = pr[..., None] * br[:, :, None] - pi[..., None] * bi[:, :, None]
        im = pr[..., None] * bi[:, :, None] + pi[..., None] * br[:, :, None]
        to_rows = lambda x: x.transpose(0, 1, 2, 4, 3).reshape(ly, g, q * gc, p)
        return to_rows(re), to_rows(im)

    pf_re, pf_im = inj(0, q - 1 - jnp.arange(q))
    pb_re, pb_im = inj(1, jnp.arange(q))
    p_m = jnp.concatenate([pf_re, pb_re, pf_im, pb_im], axis=-1)

    def carry(d, powers):
        re = jnp.take(cl_re[:, d], powers, axis=2)
        im = jnp.take(cl_im[:, d], powers, axis=2)
        to_cols = lambda x: x.transpose(0, 1, 4, 2, 3).reshape(ly, g, p, q * gc)
        return to_cols(re), to_cols(-im)

    rf_re, rf_im = carry(0, jnp.arange(q) + 1)
    rb_re, rb_im = carry(1, q - jnp.arange(q))
    zr = jnp.zeros_like(rf_re)
    rf_m = jnp.concatenate([rf_re, zr, rf_im, zr], axis=2)
    rb_m = jnp.concatenate([zr, rb_re, zr, rb_im], axis=2)
    a_m = jnp.concatenate([pw_re[:, 0, :, q], pw_re[:, 1, :, q], pw_im[:, 0, :, q], pw_im[:, 1, :, q]], axis=-1)
    return t_m.astype(BF16), p_m.astype(BF16), rf_m.astype(BF16), rb_m.astype(BF16), a_m[:, :, None, :]


def _prepare_params(w_mod, b_mod, w_in, w_uq, w_ukv, w_attn_o, w_conv_o, w_glu, w_ssm_o, w_out, w_ff1, w_ff2,
                    q_lora, kv_lora, conv_w, ssm_w):
    ly = w_in.shape[0]
    swap = _rope_swap_index()
    o = [0, q_lora, q_lora + kv_lora, q_lora + kv_lora + ROPE_DIM]
    o += [o[3] + conv_w, o[3] + 2 * conv_w, o[3] + 2 * conv_w + ssm_w]
    cols = lambda a, b: w_in[:, :, a:b].astype(BF16)
    kpe = cols(o[2], o[3])
    qk = NOPE_DIM + ROPE_DIM
    uq = w_uq.astype(BF16).reshape(ly, q_lora, N_HEADS, qk)
    pe = uq[..., NOPE_DIM:]
    ukv = w_ukv.astype(BF16).reshape(ly, kv_lora, N_HEADS, NOPE_DIM + V_DIM)
    return dict(
        w_mod=w_mod.astype(BF16), b_mod=b_mod[:, None, :],
        w_a=jnp.concatenate([cols(0, o[2]), kpe, kpe[:, :, swap]], axis=2),
        w_conv_a=cols(o[3], o[4]), w_conv_b=cols(o[4], o[5]),
        w_ssm=cols(o[5], o[6]), w_gate=cols(o[6], w_in.shape[2]),
        w_uq=jnp.concatenate([uq[..., :NOPE_DIM], pe, pe[..., swap]], axis=-1).reshape(ly, q_lora, N_HEADS * HEAD_W),
        w_uk=ukv[..., :NOPE_DIM].reshape(ly, kv_lora, N_HEADS * NOPE_DIM),
        w_uv=ukv[..., NOPE_DIM:].reshape(ly, kv_lora, N_HEADS * V_DIM),
        w_attn_o=w_attn_o.astype(BF16), w_conv_o=w_conv_o.astype(BF16), w_glu=w_glu.astype(BF16),
        w_ssm_o=w_ssm_o.astype(BF16), w_out=w_out.astype(BF16), w_ff1=w_ff1.astype(BF16),
        w_ff2=w_ff2.astype(BF16),
    )


def _block(x, mod, pp, vec, ssm_m, layer, batch, seq_len, rope_tabs, cache):
    shift1, scale1, gate1, shift2, scale2, gate2 = mod
    t = x.shape[0]
    cc, ss = rope_tabs
    norm_in = (x, vec["g_mix_pre"], scale1, shift1)

    q, ckv, kpe2 = _attn_in(*norm_in, pp["w_a"], vec["g_q"], vec["g_kv"], pp["w_uq"], cc, ss, layer, seq_len)
    k_new, v_new = _kv_build(ckv, kpe2, cc, ss, pp["w_uk"], pp["w_uv"], layer)
    shape3 = lambda a, b: a.reshape(b, a.shape[0] // b, a.shape[1])
    kvs = [(shape3(k_new, batch), shape3(v_new, batch))]
    if cache is not None:
        ckv_ctx, kpe_ctx, icc, iss = cache
        k_ctx, v_ctx = _kv_build(ckv_ctx, kpe_ctx, icc, iss, pp["w_uk"], pp["w_uv"], layer)
        kvs.append((shape3(k_ctx, batch), shape3(v_ctx, batch)))
    heads_per_step = min(N_HEADS, 8 if seq_len <= 256 else 4)
    ao = _attention(shape3(q, batch), kvs, heads_per_step).reshape(t, N_HEADS * V_DIM)

    z = _normmm(*norm_in, [pp["w_conv_a"], pp["w_conv_b"]], layer, seq_len, "glu")
    cv = _conv_branch(z, vec["conv_w"], vec["conv_b"], layer, batch, seq_len)

    u = _normmm(*norm_in, [pp["w_ssm"]], layer, seq_len, "none", out_dtype=F32)
    y, fin = _ssm_scan(u, ssm_m["h0"], ssm_m["t"], ssm_m["p"], ssm_m["rf"], ssm_m["rb"], ssm_m["a"], layer, batch)
    sa = _ssm_post(y, pp["w_glu"], vec["b_glu"], layer)

    gates = _normmm(*norm_in, [pp["w_gate"]], layer, seq_len, "sigmoid")
    x = _branch_out(ao, cv, sa, gates, x, pp["w_attn_o"], pp["w_conv_o"], pp["w_ssm_o"], pp["w_out"],
                    vec["conv_ln_g"], vec["conv_ln_b"], vec["g_mix_post"], gate1, layer, seq_len)
    x = _mlp(x, vec["g_mlp_pre"], scale2, shift2, pp["w_ff1"], pp["w_ff2"], vec["g_mlp_post"], gate2, layer,
             seq_len)
    return x, ckv, kpe2[:, :ROPE_DIM], fin


def kernel(x_prompt, x_sample, cache_ckv, cache_kpe, state_ssm_re, state_ssm_im, c, c_ctx, w_mod, b_mod, g_mix_pre, g_mix_post, g_mlp_pre, g_mlp_post, w_in, g_q, w_uq, g_kv, w_ukv, w_attn_o, conv_w, conv_b, conv_ln_g, conv_ln_b, w_conv_o, ssm_a_re, ssm_a_im, ssm_log_dt, ssm_b_re, ssm_b_im, ssm_c_re, ssm_c_im, ssm_d, w_glu, b_glu, w_ssm_o, w_out, w_ff1, w_ff2):
    bp, lp, d = x_prompt.shape
    bs, ls, _ = x_sample.shape
    depth = w_in.shape[0]
    past = cache_ckv.shape[2]
    q_lora, kv_lora = g_q.shape[1], g_kv.shape[1]
    conv_ch, ssm_ch = conv_w.shape[2], ssm_d.shape[1]
    n_groups = ssm_ch // SSM_GROUP_CH

    pp = _prepare_params(w_mod, b_mod, w_in, w_uq, w_ukv, w_attn_o, w_conv_o, w_glu, w_ssm_o, w_out, w_ff1,
                         w_ff2, q_lora, kv_lora, conv_ch, ssm_ch)
    row = lambda a: a[:, None, :]
    vec = dict(g_mix_pre=row(g_mix_pre), g_mix_post=row(g_mix_post), g_mlp_pre=row(g_mlp_pre),
               g_mlp_post=row(g_mlp_post), g_q=row(g_q), g_kv=row(g_kv), conv_w=conv_w, conv_b=row(conv_b),
               conv_ln_g=row(conv_ln_g), conv_ln_b=row(conv_ln_b), b_glu=row(b_glu))
    t_m, p_m, rf_m, rb_m, a_m = _ssm_matrices(ssm_a_re, ssm_a_im, ssm_log_dt, ssm_b_re, ssm_b_im, ssm_c_re, ssm_c_im,
                                       ssm_d)

    n_cond = 1 + bs
    cond = jnp.concatenate([c_ctx[None, :], c, jnp.zeros((-n_cond % 8, d), F32)], axis=0)
    mod_all = _modulation(cond, pp["w_mod"], pp["b_mod"])

    rope_s = _rope_tables(ls)
    ident_p = _identity_tables(min(lp, 256))
    ident_c = _identity_tables(min(past, 256))
    zero_h0 = jnp.zeros((n_groups, bp, 4 * SSM_STATE), F32)

    xp = x_prompt.reshape(bp * lp, d)
    xs = x_sample.reshape(bs * ls, d)
    ckvs, kpes, fins = [], [], []
    for l in range(depth):
        mods = jnp.split(mod_all[l], 6, axis=-1)
        mod_ctx = [m[0:1, None, :] for m in mods]
        mod_lat = [m[1:n_cond, None, :] for m in mods]
        ssm_p = dict(t=t_m, p=p_m, rf=rf_m, rb=rb_m, a=a_m, h0=zero_h0)
        xp, ckv, kpe, fin = _block(xp, mod_ctx, pp, vec, ssm_p, l, bp, lp, ident_p, None)
        ckvs.append(ckv.reshape(bp, lp, kv_lora))
        kpes.append(kpe.reshape(bp, lp, ROPE_DIM))
        fins.append(fin)
        sr, si = state_ssm_re[:, l], state_ssm_im[:, l]
        h0 = jnp.concatenate([sr[:, 0], sr[:, 1], si[:, 0], si[:, 1]], axis=-1).transpose(1, 0, 2)
        kpe_ctx = jnp.pad(cache_kpe[:, l].reshape(bs * past, ROPE_DIM), ((0, 0), (0, ROPE_DIM)))
        cache = (cache_ckv[:, l].reshape(bs * past, kv_lora), kpe_ctx) + ident_c
        ssm_s = dict(t=t_m, p=p_m, rf=rf_m, rb=rb_m, a=a_m, h0=h0)
        xs = _block(xs, mod_lat, pp, vec, ssm_s, l, bs, ls, rope_s, cache)[0]

    new_ckv = jnp.stack(ckvs, axis=1)
    new_kpe = jnp.stack(kpes, axis=1)
    fin = jnp.stack(fins, axis=0)
    fin = fin.reshape(depth, n_groups, bp, 2, 2, SSM_STATE).transpose(3, 2, 0, 4, 1, 5)
    return (xp.reshape(bp, lp, d), xs.reshape(bs, ls, d), new_ckv, new_kpe, fin[0], fin[1])
```

```python
import functools
import math

import jax
import jax.numpy as jnp
from jax import lax
from jax.experimental import pallas as pl
from jax.experimental.pallas import tpu as pltpu

N_HEADS = 16
NOPE_DIM = 128
ROPE_DIM = 64
V_DIM = 128
GRID_W = 64
ROPE_BASE = 10000.0
SSM_GROUP_CH = 16
SSM_STATE = 64
EPS = 1e-6

HEAD_W = NOPE_DIM + 2 * ROPE_DIM
SSM_CHUNK = 16
V7X_VMEM_LIMIT_BYTES = 56 * 1024 * 1024
_LANES = 128

F32 = jnp.float32
BF16 = jnp.bfloat16


def _cparams(*sem):
    return pltpu.CompilerParams(dimension_semantics=sem, vmem_limit_bytes=V7X_VMEM_LIMIT_BYTES)


def _tile(n, target):
    if n <= target:
        return n
    for t in range(target, 7, -1):
        if n % t == 0 and t % 8 == 0:
            return t
    return n


def _rms(x, g):
    return x * lax.rsqrt(jnp.mean(x * x, axis=-1, keepdims=True) + EPS) * g


def _rmsmod(x, g, scale, shift):
    return _rms(x, g) * (1.0 + scale) + shift


def _mod_kernel(c_ref, w_ref, b_ref, o_ref):
    h = jax.nn.silu(c_ref[...]).astype(BF16)
    o_ref[...] = jnp.dot(h, w_ref[...].astype(BF16), preferred_element_type=F32) + b_ref[...]


def _modulation(cond, w_mod, b_mod):
    ly, d, n = w_mod.shape
    r = cond.shape[0]
    tn = _tile(n, 1536)
    return pl.pallas_call(
        _mod_kernel,
        grid=(ly, n // tn),
        in_specs=[
            pl.BlockSpec((r, d), lambda l, j: (0, 0)),
            pl.BlockSpec((None, d, tn), lambda l, j: (l, 0, j)),
            pl.BlockSpec((None, 1, tn), lambda l, j: (l, 0, j)),
        ],
        out_specs=pl.BlockSpec((None, r, tn), lambda l, j: (l, 0, j)),
        out_shape=jax.ShapeDtypeStruct((ly, r, n), F32),
        compiler_params=_cparams("parallel", "parallel"),
        name="modulation",
    )(cond, w_mod, b_mod)


def _proj_kernel(h_ref, *rest, n_w, epilogue):
    w_refs, o_ref = rest[:n_w], rest[n_w]
    h = h_ref[...]
    a = jnp.dot(h, w_refs[0][...], preferred_element_type=F32)
    if epilogue == "glu":
        a = a * jax.nn.sigmoid(jnp.dot(h, w_refs[1][...], preferred_element_type=F32))
    elif epilogue == "sigmoid":
        a = jax.nn.sigmoid(a)
    o_ref[...] = a.astype(o_ref.dtype)


def _proj(h, w, col_starts, n, layer, epilogue, out_dtype=BF16, tm=1024, tn=1024):
    t, d = h.shape
    tm = _tile(t, tm)
    tn = next(c for c in range(min(n, tn // len(col_starts)), 0, -_LANES)
              if n % c == 0 and all(s % c == 0 for s in col_starts))
    w_specs = [pl.BlockSpec((None, d, tn), functools.partial(lambda i, j, off: (layer, 0, off + j), off=c // tn))
               for c in col_starts]
    return pl.pallas_call(
        functools.partial(_proj_kernel, n_w=len(col_starts), epilogue=epilogue),
        grid=(t // tm, n // tn),
        in_specs=[pl.BlockSpec((tm, d), lambda i, j: (i, 0))] + w_specs,
        out_specs=pl.BlockSpec((tm, tn), lambda i, j: (i, j)),
        out_shape=jax.ShapeDtypeStruct((t, n), out_dtype),
        compiler_params=_cparams("parallel", "parallel"),
        name="proj_" + epilogue,
    )(h, *([w] * len(col_starts)))


def _attn_in_kernel(x_ref, g_ref, sc_ref, sh_ref, wa_ref, gq_ref, gkv_ref, wuq_ref, cc_ref, ss_ref,
                    q_ref, ckv_ref, kpe_ref, h_ref, *, q_lora, kv_lora):
    h = _rmsmod(x_ref[...], g_ref[...], sc_ref[0], sh_ref[0]).astype(BF16)
    h_ref[...] = h
    a = jnp.dot(h, wa_ref[...], preferred_element_type=F32)
    cq = _rms(a[:, :q_lora], gq_ref[...]).astype(BF16)
    ckv_ref[...] = _rms(a[:, q_lora:q_lora + kv_lora], gkv_ref[...])
    kpe_ref[...] = a[:, q_lora + kv_lora:]
    q = jnp.dot(cq, wuq_ref[...], preferred_element_type=F32)
    cc, ss = cc_ref[...], ss_ref[...]
    for hd in range(N_HEADS):
        lo = hd * HEAD_W
        q_ref[:, lo:lo + NOPE_DIM] = q[:, lo:lo + NOPE_DIM].astype(BF16)
        v = q[:, lo + NOPE_DIM:lo + HEAD_W]
        q_ref[:, lo + NOPE_DIM:lo + HEAD_W] = (v * cc + pltpu.roll(v, ROPE_DIM, 1) * ss).astype(BF16)


def _attn_in(x, g, scale, shift, w_a, g_q, g_kv, w_uq, cc, ss, layer, seq_len, tm=256):
    t, d = x.shape
    na = w_a.shape[2]
    q_lora, kv_lora = g_q.shape[2], g_kv.shape[2]
    nq = w_uq.shape[2]
    tm = _tile(seq_len, tm)
    per_seq = seq_len // tm
    nb = scale.shape[0]
    bidx = (lambda i: i // per_seq) if nb > 1 else (lambda i: 0)
    ntab = cc.shape[0] // tm
    lsel = lambda i: (layer, 0, 0)
    return pl.pallas_call(
        functools.partial(_attn_in_kernel, q_lora=q_lora, kv_lora=kv_lora),
        grid=(t // tm,),
        in_specs=[
            pl.BlockSpec((tm, d), lambda i: (i, 0)),
            pl.BlockSpec((None, 1, d), lsel),
            pl.BlockSpec((1, 1, d), lambda i: (bidx(i), 0, 0)),
            pl.BlockSpec((1, 1, d), lambda i: (bidx(i), 0, 0)),
            pl.BlockSpec((None, d, na), lsel),
            pl.BlockSpec((None, 1, q_lora), lsel),
            pl.BlockSpec((None, 1, kv_lora), lsel),
            pl.BlockSpec((None, q_lora, nq), lsel),
            pl.BlockSpec((tm, 2 * ROPE_DIM), lambda i: (i % ntab, 0)),
            pl.BlockSpec((tm, 2 * ROPE_DIM), lambda i: (i % ntab, 0)),
        ],
        out_specs=[
            pl.BlockSpec((tm, nq), lambda i: (i, 0)),
            pl.BlockSpec((tm, kv_lora), lambda i: (i, 0)),
            pl.BlockSpec((tm, 2 * ROPE_DIM), lambda i: (i, 0)),
            pl.BlockSpec((tm, d), lambda i: (i, 0)),
        ],
        out_shape=[
            jax.ShapeDtypeStruct((t, nq), BF16),
            jax.ShapeDtypeStruct((t, kv_lora), F32),
            jax.ShapeDtypeStruct((t, 2 * ROPE_DIM), F32),
            jax.ShapeDtypeStruct((t, d), BF16),
        ],
        compiler_params=_cparams("parallel"),
        name="attn_in",
    )(x, g, scale, shift, w_a, g_q, g_kv, w_uq, cc, ss)


def _kv_build_kernel(ckv_ref, kpe_ref, cc_ref, ss_ref, wuk_ref, wuv_ref, k_ref, v_ref):
    c = ckv_ref[...].astype(BF16)
    kn = jnp.dot(c, wuk_ref[...], preferred_element_type=F32)
    v_ref[...] = jnp.dot(c, wuv_ref[...], preferred_element_type=F32).astype(BF16)
    kp = kpe_ref[...]
    kp = (kp * cc_ref[...] + pltpu.roll(kp, ROPE_DIM, 1) * ss_ref[...]).astype(BF16)
    for hd in range(N_HEADS):
        k_ref[:, hd * HEAD_W:hd * HEAD_W + NOPE_DIM] = kn[:, hd * NOPE_DIM:(hd + 1) * NOPE_DIM].astype(BF16)
        k_ref[:, hd * HEAD_W + NOPE_DIM:(hd + 1) * HEAD_W] = kp


def _kv_build(ckv, kpe2, cc, ss, w_uk, w_uv, layer, tm=256):
    t, kvl = ckv.shape
    tm = _tile(cc.shape[0], tm)
    ntab = cc.shape[0] // tm
    nk, nv = N_HEADS * HEAD_W, w_uv.shape[2]
    lsel = lambda i: (layer, 0, 0)
    return pl.pallas_call(
        _kv_build_kernel,
        grid=(t // tm,),
        in_specs=[
            pl.BlockSpec((tm, kvl), lambda i: (i, 0)),
            pl.BlockSpec((tm, 2 * ROPE_DIM), lambda i: (i, 0)),
            pl.BlockSpec((tm, 2 * ROPE_DIM), lambda i: (i % ntab, 0)),
            pl.BlockSpec((tm, 2 * ROPE_DIM), lambda i: (i % ntab, 0)),
            pl.BlockSpec((None, kvl, w_uk.shape[2]), lsel),
            pl.BlockSpec((None, kvl, nv), lsel),
        ],
        out_specs=[pl.BlockSpec((tm, nk), lambda i: (i, 0)), pl.BlockSpec((tm, nv), lambda i: (i, 0))],
        out_shape=[jax.ShapeDtypeStruct((t, nk), BF16), jax.ShapeDtypeStruct((t, nv), BF16)],
        compiler_params=_cparams("parallel"),
        name="kv_build",
    )(ckv, kpe2, cc, ss, w_uk, w_uv)


def _attn_kernel(q_ref, *rest, n_src, heads):
    kv_refs, o_ref = rest[:2 * n_src], rest[2 * n_src]
    scale = 1.0 / math.sqrt(NOPE_DIM + ROPE_DIM)
    for hd in range(heads):
        q = q_ref[:, hd * HEAD_W:(hd + 1) * HEAD_W]
        s = [lax.dot_general(q, kv_refs[2 * i][:, hd * HEAD_W:(hd + 1) * HEAD_W],
                             (((1,), (1,)), ((), ())), preferred_element_type=F32) * scale
             for i in range(n_src)]
        m = jnp.max(s[0], axis=-1, keepdims=True)
        for si in s[1:]:
            m = jnp.maximum(m, jnp.max(si, axis=-1, keepdims=True))
        acc, den = None, None
        for i in range(n_src):
            p = jnp.exp(s[i] - m)
            li = jnp.sum(p, axis=-1, keepdims=True)
            oi = jnp.dot(p.astype(BF16), kv_refs[2 * i + 1][:, hd * V_DIM:(hd + 1) * V_DIM],
                         preferred_element_type=F32)
            acc = oi if acc is None else acc + oi
            den = li if den is None else den + li
        o_ref[:, hd * V_DIM:(hd + 1) * V_DIM] = (acc / den).astype(BF16)


def _attention(q, kvs, heads_per_step, tq=256):
    b, sq, _ = q.shape
    tq = _tile(sq, tq)
    hb = heads_per_step
    in_specs = [pl.BlockSpec((None, tq, hb * HEAD_W), lambda bi, hi, qi: (bi, qi, hi))]
    args = [q]
    for k, v in kvs:
        sk = k.shape[1]
        in_specs.append(pl.BlockSpec((None, sk, hb * HEAD_W), lambda bi, hi, qi: (bi, 0, hi)))
        in_specs.append(pl.BlockSpec((None, sk, hb * V_DIM), lambda bi, hi, qi: (bi, 0, hi)))
        args += [k, v]
    return pl.pallas_call(
        functools.partial(_attn_kernel, n_src=len(kvs), heads=hb),
        grid=(b, N_HEADS // hb, sq // tq),
        in_specs=in_specs,
        out_specs=pl.BlockSpec((None, tq, hb * V_DIM), lambda bi, hi, qi: (bi, qi, hi)),
        out_shape=jax.ShapeDtypeStruct((b, sq, N_HEADS * V_DIM), BF16),
        compiler_params=_cparams("parallel", "parallel", "parallel"),
        name="attention",
    )(*args)


_CONV_ROWS = 32
_CONV_LANES = 256
_CONV_HALO = 16
_SUBLANES = 8


def _conv_kernel(z_ref, w_ref, b_ref, o_ref, zs_ref, wb_ref, *, seq_len, taps):
    lanes = z_ref.shape[-1]
    pad = (taps - 1) // 2
    halo = jnp.zeros((_CONV_HALO, lanes), F32)
    zp = jnp.concatenate([halo, z_ref[...].astype(F32), halo], axis=0)
    span = seq_len + 2 * _CONV_HALO - _SUBLANES
    for j in range(_SUBLANES):
        zs_ref[j, 0:span, :] = zp[j:j + span, :]
    for k in range(taps):
        wb_ref[k] = jnp.broadcast_to(w_ref[k:k + 1, :], (_SUBLANES, lanes))
    bias = jnp.broadcast_to(b_ref[...], (_SUBLANES, lanes))
    n_sub = _CONV_ROWS // _SUBLANES

    def conv_step(it, carry):
        r0 = pl.multiple_of(it * _CONV_ROWS, _CONV_ROWS)
        acc = [bias] * n_sub
        for k in range(taps):
            off = _CONV_HALO - pad + k
            j, base = off % _SUBLANES, off - off % _SUBLANES
            win = zs_ref[j, pl.ds(r0 + base, _CONV_ROWS), :]
            wk = wb_ref[k]
            acc = [acc[p] + wk * win[p * _SUBLANES:(p + 1) * _SUBLANES, :] for p in range(n_sub)]
        o_ref[pl.ds(r0, _CONV_ROWS), :] = jnp.concatenate(acc, axis=0).astype(BF16)
        return carry

    lax.fori_loop(0, seq_len // _CONV_ROWS, conv_step, 0)


def _conv_branch(z, conv_w, conv_b, layer, batch, seq_len):
    t, ch = z.shape
    taps = conv_w.shape[1]
    lanes = _tile(ch, _CONV_LANES)
    return pl.pallas_call(
        functools.partial(_conv_kernel, seq_len=seq_len, taps=taps),
        grid=(batch, ch // lanes),
        in_specs=[
            pl.BlockSpec((seq_len, lanes), lambda b, c: (b, c)),
            pl.BlockSpec((None, taps, lanes), lambda b, c: (layer, 0, c)),
            pl.BlockSpec((None, 1, lanes), lambda b, c: (layer, 0, c)),
        ],
        out_specs=pl.BlockSpec((seq_len, lanes), lambda b, c: (b, c)),
        out_shape=jax.ShapeDtypeStruct((t, ch), BF16),
        scratch_shapes=[pltpu.VMEM((_SUBLANES, seq_len + 2 * _CONV_HALO, lanes), F32),
                        pltpu.VMEM((taps, _SUBLANES, lanes), F32)],
        compiler_params=_cparams("parallel", "parallel"),
        name="depthwise_conv",
    )(z, conv_w, conv_b)


_SSM_TILE_GROUPS = _LANES // SSM_GROUP_CH
_SSM_MOVE_ROWS = 64


def _ssm_kernel(x_ref, h0_ref, t_ref, p_ref, rf_ref, rb_ref, a_ref, y_ref, fin_ref,
                u_scr, y_scr, ds_scr, sf_scr, sb_scr, *, n_chunks, batch):
    m = n_chunks * batch
    ng, q, gc = _SSM_TILE_GROUPS, SSM_CHUNK, SSM_GROUP_CH
    half = 2 * SSM_STATE
    rr = min(_SSM_MOVE_ROWS, m)
    blk = lax.broadcasted_iota(jnp.int32, (rr, _LANES), 1) // gc
    is_fwd = lax.broadcasted_iota(jnp.int32, (batch, half), 1) < SSM_STATE

    def block_transpose(vs):
        vs = list(vs)
        d = ng // 2
        while d >= 1:
            odd = (blk // d) % 2 == 1
            nxt = list(vs)
            for a in range(ng):
                if (a // d) % 2 == 0:
                    lo, hi = vs[a], vs[a + d]
                    nxt[a] = jnp.where(odd, pltpu.roll(hi, d * gc, 1), lo)
                    nxt[a + d] = jnp.where(odd, hi, pltpu.roll(lo, _LANES - d * gc, 1))
            vs = nxt
            d //= 2
        return vs

    def gather_in(it, carry):
        r0 = pl.multiple_of(it * rr, rr)
        halves = [block_transpose([x_ref[pl.ds(r0, rr), h * ng + j, :] for j in range(ng)])
                  for h in range(q // ng)]
        for g in range(ng):
            u_scr[g, pl.ds(r0, rr), :] = jnp.concatenate([hv[g] for hv in halves], axis=1).astype(BF16)
        return carry

    lax.fori_loop(0, m // rr, gather_in, 0)

    def one_group(g, carry):
        u = u_scr[g]
        ds_scr[...] = jnp.dot(u, p_ref[g], preferred_element_type=F32).reshape(batch, n_chunks, 2 * half)
        a = a_ref[g]
        h0 = h0_ref[g]
        a_re, a_im = a[:, :half], a[:, half:]
        s_re, s_im = h0[:, :half], h0[:, half:]
        for c in range(n_chunks):
            cb = n_chunks - 1 - c
            sf_scr[:, c, :half] = s_re
            sf_scr[:, c, half:] = s_im
            sb_scr[:, cb, :half] = s_re
            sb_scr[:, cb, half:] = s_im
            d_re = jnp.where(is_fwd, ds_scr[:, c, :half], ds_scr[:, cb, :half])
            d_im = jnp.where(is_fwd, ds_scr[:, c, half:], ds_scr[:, cb, half:])
            s_re, s_im = (a_re * s_re - a_im * s_im + d_re, a_re * s_im + a_im * s_re + d_im)
        fin_ref[g] = jnp.concatenate([s_re, s_im], axis=1)
        y = jnp.dot(u, t_ref[g], preferred_element_type=F32)
        y = y + jnp.dot(sf_scr[...].reshape(m, 2 * half).astype(BF16), rf_ref[g], preferred_element_type=F32)
        y = y + jnp.dot(sb_scr[...].reshape(m, 2 * half).astype(BF16), rb_ref[g], preferred_element_type=F32)
        y_scr[g] = y
        return carry

    lax.fori_loop(0, ng, one_group, 0)

    def scatter_out(it, carry):
        r0 = pl.multiple_of(it * rr, rr)
        ys = [y_scr[g, pl.ds(r0, rr), :] for g in range(ng)]
        for h in range(q // ng):
            toks = block_transpose([y[:, h * _LANES:(h + 1) * _LANES] for y in ys])
            for j in range(ng):
                y_ref[pl.ds(r0, rr), h * ng + j, :] = toks[j]
        return carry

    lax.fori_loop(0, m // rr, scatter_out, 0)


def _ssm_scan(u, h0, t_m, p_m, rf_m, rb_m, a_m, layer, batch):
    t, ch = u.shape
    m = t // SSM_CHUNK
    n_chunks = m // batch
    ng, w = _SSM_TILE_GROUPS, SSM_CHUNK * SSM_GROUP_CH
    g_all = ch // SSM_GROUP_CH
    tok = pl.BlockSpec((m, SSM_CHUNK, _LANES), lambda i: (0, 0, i))
    wsel = lambda i: (layer, i, 0, 0)
    mat = pl.BlockSpec((None, ng, w, w), wsel)
    st = pl.BlockSpec((ng, batch, w), lambda i: (i, 0, 0))
    y, fin = pl.pallas_call(
        functools.partial(_ssm_kernel, n_chunks=n_chunks, batch=batch),
        grid=(g_all // ng,),
        in_specs=[tok, st, mat, mat, mat, mat, pl.BlockSpec((None, ng, 1, w), wsel)],
        out_specs=[tok, st],
        out_shape=[jax.ShapeDtypeStruct((m, SSM_CHUNK, ch), F32), jax.ShapeDtypeStruct((g_all, batch, w), F32)],
        scratch_shapes=[pltpu.VMEM((ng, m, w), BF16), pltpu.VMEM((ng, m, w), F32),
                        pltpu.VMEM((batch, n_chunks, w), F32), pltpu.VMEM((batch, n_chunks, w), F32),
                        pltpu.VMEM((batch, n_chunks, w), F32)],
        compiler_params=_cparams("parallel"),
        name="ssm_chunk_scan",
    )(u.reshape(m, SSM_CHUNK, ch), h0, t_m, p_m, rf_m, rb_m, a_m)
    return y.reshape(t, ch), fin


def _ssm_post_kernel(y_ref, w_ref, b_ref, o_ref):
    g = jax.nn.gelu(y_ref[...].astype(F32))
    s = jnp.dot(g.astype(BF16), w_ref[...], preferred_element_type=F32) + b_ref[...]
    o_ref[...] = (g * jax.nn.sigmoid(s)).astype(BF16)


def _ssm_post(y, w_glu, b_glu, layer, tm=512):
    t, w = y.shape
    tm = _tile(t, tm)
    lsel = lambda i: (layer, 0, 0)
    return pl.pallas_call(
        _ssm_post_kernel,
        grid=(t // tm,),
        in_specs=[pl.BlockSpec((tm, w), lambda i: (i, 0)), pl.BlockSpec((None, w, w), lsel),
                  pl.BlockSpec((None, 1, w), lsel)],
        out_specs=pl.BlockSpec((tm, w), lambda i: (i, 0)),
        out_shape=jax.ShapeDtypeStruct((t, w), BF16),
        compiler_params=_cparams("parallel"),
        name="ssm_gelu_glu",
    )(y, w_glu, b_glu)


def _accumulate_then(acc_ref, part, k, nk, finish):
    if nk == 1:
        finish(part)
        return

    @pl.when(k == 0)
    def _():
        acc_ref[...] = part

    @pl.when(jnp.logical_and(k > 0, k < nk - 1))
    def _():
        acc_ref[...] += part

    @pl.when(k == nk - 1)
    def _():
        finish(acc_ref[...] + part)


def _branch_out_kernel(ao_ref, cv_ref, sa_ref, ga_ref, gc_ref, gs_ref, wa_ref, wc_ref, ws_ref, wo_ref,
                       lg_ref, lb_ref, g_ref, gate_ref, x_ref, o_ref, ca_ref, acc_ref, *, nj):
    j = pl.program_id(1)

    @pl.when(j == 0)
    def _():
        v = cv_ref[...].astype(F32)
        vc = v - jnp.mean(v, axis=-1, keepdims=True)
        var = jnp.mean(vc * vc, axis=-1, keepdims=True)
        y = vc * lax.rsqrt(var + EPS) * lg_ref[...] + lb_ref[...]
        ca_ref[...] = jax.nn.silu(y).astype(BF16)

    a = jnp.dot(ao_ref[...], wa_ref[...], preferred_element_type=F32)
    c = jnp.dot(ca_ref[...], wc_ref[...], preferred_element_type=F32)
    s = jnp.dot(sa_ref[...], ws_ref[...], preferred_element_type=F32)
    mix = ga_ref[...].astype(F32) * a + gc_ref[...].astype(F32) * c + gs_ref[...].astype(F32) * s
    part = jnp.dot(mix.astype(BF16), wo_ref[...], preferred_element_type=F32)

    def finish(total):
        o_ref[...] = x_ref[...] + gate_ref[0] * _rms(total, g_ref[...])

    _accumulate_then(acc_ref, part, j, nj, finish)


def _branch_out(ao, cv, sa, gates, x, w_attn_o, w_conv_o, w_ssm_o, w_out, ln_g, ln_b, g_post, gate, layer,
                seq_len, tm=512, tn=512):
    t, d = x.shape
    nb = gate.shape[0]
    tm, tn = _tile(t if nb == 1 else seq_len, tm), _tile(d, tn)
    nj = d // tn
    per_seq = seq_len // tm
    bidx = (lambda i: i // per_seq) if nb > 1 else (lambda i: 0)
    act = lambda a: pl.BlockSpec((tm, a.shape[1]), lambda i, j: (i, 0))
    wcol = lambda w: pl.BlockSpec((None, w.shape[1], tn), lambda i, j: (layer, 0, j))
    gcol = lambda k: pl.BlockSpec((tm, tn), lambda i, j: (i, k * nj + j))
    vec = lambda v: pl.BlockSpec((None, 1, v.shape[2]), lambda i, j: (layer, 0, 0))
    return pl.pallas_call(
        functools.partial(_branch_out_kernel, nj=nj),
        grid=(t // tm, nj),
        in_specs=[act(ao), act(cv), act(sa), gcol(0), gcol(1), gcol(2),
                  wcol(w_attn_o), wcol(w_conv_o), wcol(w_ssm_o),
                  pl.BlockSpec((None, tn, d), lambda i, j: (layer, j, 0)),
                  vec(ln_g), vec(ln_b), vec(g_post),
                  pl.BlockSpec((1, 1, d), lambda i, j: (bidx(i), 0, 0)),
                  pl.BlockSpec((tm, d), lambda i, j: (i, 0))],
        out_specs=pl.BlockSpec((tm, d), lambda i, j: (i, 0)),
        out_shape=jax.ShapeDtypeStruct((t, d), F32),
        scratch_shapes=[pltpu.VMEM((tm, cv.shape[1]), BF16), pltpu.VMEM((tm, d), F32)],
        compiler_params=_cparams("parallel", "arbitrary"),
        name="branch_out",
    )(ao, cv, sa, gates, gates, gates, w_attn_o, w_conv_o, w_ssm_o, w_out, ln_g, ln_b, g_post, gate, x)


def _mlp_kernel(x_ref, gpre_ref, sc_ref, sh_ref, w1_ref, w2_ref, gpost_ref, gate_ref, o_ref, h_ref, acc_ref, *,
                nk):
    k = pl.program_id(1)

    @pl.when(k == 0)
    def _():
        h_ref[...] = _rmsmod(x_ref[...], gpre_ref[...], sc_ref[0], sh_ref[0]).astype(BF16)

    f = jnp.square(jnp.maximum(jnp.dot(h_ref[...], w1_ref[...], preferred_element_type=F32), 0.0))
    part = jnp.dot(f.astype(BF16), w2_ref[...], preferred_element_type=F32)

    def finish(total):
        o_ref[...] = x_ref[...] + gate_ref[0] * _rms(total, gpost_ref[...])

    _accumulate_then(acc_ref, part, k, nk, finish)


def _mlp(x, g_pre, scale, shift, w1, w2, g_post, gate, layer, seq_len, tm=512, tf=1024):
    t, d = x.shape
    ff = w1.shape[2]
    nb = gate.shape[0]
    tm, tf = _tile(t if nb == 1 else seq_len, tm), _tile(ff, tf)
    nk = ff // tf
    per_seq = seq_len // tm
    bidx = (lambda i: i // per_seq) if nb > 1 else (lambda i: 0)
    vec = pl.BlockSpec((None, 1, d), lambda i, k: (layer, 0, 0))
    mod = pl.BlockSpec((1, 1, d), lambda i, k: (bidx(i), 0, 0))
    return pl.pallas_call(
        functools.partial(_mlp_kernel, nk=nk),
        grid=(t // tm, nk),
        in_specs=[pl.BlockSpec((tm, d), lambda i, k: (i, 0)), vec, mod, mod,
                  pl.BlockSpec((None, d, tf), lambda i, k: (layer, 0, k)),
                  pl.BlockSpec((None, tf, d), lambda i, k: (layer, k, 0)),
                  vec, mod],
        out_specs=pl.BlockSpec((tm, d), lambda i, k: (i, 0)),
        out_shape=jax.ShapeDtypeStruct((t, d), F32),
        scratch_shapes=[pltpu.VMEM((tm, d), BF16), pltpu.VMEM((tm, d), F32)],
        compiler_params=_cparams("parallel", "arbitrary"),
        name="mlp_relu2",
    )(x, g_pre, scale, shift, w1, w2, g_post, gate)


def _rope_swap_index():
    q = ROPE_DIM // 4
    return jnp.concatenate([jnp.arange(q, 2 * q), jnp.arange(0, q), jnp.arange(3 * q, 4 * q), jnp.arange(2 * q, 3 * q)])


def _rope_tables(n_tok):
    pairs = ROPE_DIM // 4
    pos = jnp.arange(n_tok)
    row = (pos // GRID_W).astype(F32)
    col = (pos % GRID_W).astype(F32)
    inv = ROPE_BASE ** (-jnp.arange(pairs, dtype=F32) / pairs)
    ang = jnp.stack([row[:, None] * inv, col[:, None] * inv], axis=1)
    cos, sin = jnp.cos(ang), jnp.sin(ang)
    cc = jnp.stack([cos, cos], axis=2).reshape(n_tok, ROPE_DIM)
    ss = jnp.stack([-sin, sin], axis=2).reshape(n_tok, ROPE_DIM)
    z = jnp.zeros((n_tok, ROPE_DIM), F32)
    return jnp.concatenate([cc, z], axis=1), jnp.concatenate([ss, z], axis=1)


def _identity_tables(n_tok):
    one = jnp.ones((n_tok, ROPE_DIM), F32)
    z = jnp.zeros((n_tok, ROPE_DIM), F32)
    return jnp.concatenate([one, z], axis=1), jnp.zeros((n_tok, 2 * ROPE_DIM), F32)


_SSM_MAT_GROUPS = 8


def _ssm_mats_kernel(a1_ref, a2_ref, ai_ref, ldt_ref, bx_ref, by_ref, cx_ref, cy_ref, d_ref,
                     t_ref, p_ref, rf_ref, rb_ref, lam_ref, *, groups):
    q, gc, ns = SSM_CHUNK, SSM_GROUP_CH, SSM_STATE
    w = q * gc
    hp = lax.Precision.HIGHEST
    n_pow = 24
    tau = lax.broadcasted_iota(jnp.int32, (n_pow, 2 * ns), 0).astype(F32)
    first = lax.broadcasted_iota(jnp.int32, (1, 2 * ns), 1) < ns
    quarter = jnp.where(first, 0.0, 0.5 * math.pi)
    one_zero = jnp.where(first, 1.0, 0.0)
    lane_w = lax.broadcasted_iota(jnp.int32, (gc, w), 1)
    eye = (lax.broadcasted_iota(jnp.int32, (gc, gc), 0) == lax.broadcasted_iota(jnp.int32, (gc, gc), 1))
    swap = lambda x: pltpu.roll(x, ns, 1)
    nt_dot = lambda x, y: lax.dot_general(x, y, (((1,), (1,)), ((), ())), precision=hp,
                                          preferred_element_type=F32)

    def direction(d, g):
        dt = jnp.exp(ldt_ref[d, g])
        a1, a2 = a1_ref[d, g], a2_ref[d, g]
        pw = jnp.exp(a1 * dt * tau) * jnp.cos(ai_ref[d, g] * dt * tau - quarter)
        pws = swap(pw)
        pw_rr = jnp.where(first, pw, pws)
        pw_ii = jnp.where(first, -pws, pw)
        num = pw[1:2] - one_zero
        den = a1 * a1 + a2 * a2
        f = (num * a1 + swap(num) * a2) / den
        bb = bx_ref[d, g] * f + by_ref[d, g] * swap(f)
        cx, cy = cx_ref[d, g], cy_ref[d, g]
        cl = [cx * pw[t:t + 1] + cy * pws[t:t + 1] for t in range(q + 1)]
        inj = lambda t: bb * pw_rr[t:t + 1] + swap(bb) * pw_ii[t:t + 1]
        return pw, bb, cl, inj

    def interleave(x_f, x_b):
        return jnp.concatenate([jnp.where(first, x_f, swap(x_b)), jnp.where(first, swap(x_f), x_b)], axis=1)

    def one_group(g, carry):
        pw_f, bb_f, cl_f, inj_f = direction(0, g)
        pw_b, bb_b, cl_b, inj_b = direction(1, g)
        krow_f = nt_dot(bb_f, jnp.concatenate(cl_f[:q], axis=0))
        krow_b = nt_dot(bb_b, jnp.concatenate(cl_b[q - 1::-1], axis=0))
        skip = jnp.where(eye, jnp.broadcast_to(d_ref[g], (gc, gc)), 0.0)
        krow_f = krow_f + jnp.concatenate([skip, jnp.zeros((gc, w - gc), F32)], axis=1)
        for s in range(q):
            fwd = krow_f if s == 0 else jnp.where(lane_w >= gc * s, pltpu.roll(krow_f, gc * s, 1), 0.0)
            k = gc * (q - 1 - s)
            bwd = krow_b if k == 0 else jnp.where(lane_w < w - k, pltpu.roll(krow_b, w - k, 1), 0.0)
            t_ref[g, s * gc:(s + 1) * gc, :] = (fwd + bwd).astype(BF16)
            p_ref[g, s * gc:(s + 1) * gc, :] = interleave(inj_f(q - 1 - s), inj_b(s)).astype(BF16)
        tr_f = jnp.concatenate(cl_f[1:q + 1], axis=0).T
        tr_b = jnp.concatenate(cl_b[q:0:-1], axis=0).T
        zero = jnp.zeros((ns, w), F32)
        rf_ref[g] = jnp.concatenate([tr_f[:ns], zero, tr_f[ns:], zero], axis=0).astype(BF16)
        rb_ref[g] = jnp.concatenate([zero, tr_b[:ns], zero, tr_b[ns:]], axis=0).astype(BF16)
        lam_ref[g] = interleave(pw_f[q:q + 1], pw_b[q:q + 1])
        return carry

    lax.fori_loop(0, groups, one_group, 0)


def _ssm_matrices(a_re, a_im, log_dt, b_re, b_im, c_re, c_im, d_skip):
    ly, _, g, p = a_re.shape
    gc, w = SSM_GROUP_CH, SSM_CHUNK * SSM_GROUP_CH
    gb = min(_SSM_MAT_GROUPS, g)
    dup = lambda x, y: jnp.concatenate([x, y], axis=-1)
    row = lambda x: x[:, :, :, None, :]
    bt_re, bt_im = jnp.swapaxes(b_re, -1, -2), jnp.swapaxes(b_im, -1, -2)
    ldt = jnp.broadcast_to(log_dt[..., None], a_re.shape)
    args = [row(dup(a_re, a_re)), row(dup(a_im, -a_im)), row(dup(a_im, a_im)), row(dup(ldt, ldt)),
            dup(bt_re, bt_re), dup(-bt_im, bt_im), dup(c_re, -c_re), dup(-c_im, -c_im),
            d_skip.reshape(ly, g, 1, gc)]
    vec = pl.BlockSpec((None, 2, gb, 1, 2 * p), lambda l, i: (l, 0, i, 0, 0))
    mat = pl.BlockSpec((None, 2, gb, gc, 2 * p), lambda l, i: (l, 0, i, 0, 0))
    out = pl.BlockSpec((None, gb, w, w), lambda l, i: (l, i, 0, 0))
    shape = jax.ShapeDtypeStruct((ly, g, w, w), BF16)
    return pl.pallas_call(
        functools.partial(_ssm_mats_kernel, groups=gb),
        grid=(ly, g // gb),
        in_specs=[vec, vec, vec, vec, mat, mat, mat, mat, pl.BlockSpec((None, gb, 1, gc), lambda l, i: (l, i, 0, 0))],
        out_specs=[out, out, out, out, pl.BlockSpec((None, gb, 1, w), lambda l, i: (l, i, 0, 0))],
        out_shape=[shape, shape, shape, shape, jax.ShapeDtypeStruct((ly, g, 1, w), F32)],
        compiler_params=_cparams("parallel", "parallel"),
        name="ssm_matrices",
    )(*args)


def _prepare_params(w_mod, b_mod, w_in, w_uq, w_ukv, w_attn_o, w_conv_o, w_glu, w_ssm_o, w_out, w_ff1, w_ff2,
                    q_lora, kv_lora, conv_w, ssm_w):
    ly = w_in.shape[0]
    swap = _rope_swap_index()
    o = [0, q_lora, q_lora + kv_lora, q_lora + kv_lora + ROPE_DIM]
    o += [o[3] + conv_w, o[3] + 2 * conv_w, o[3] + 2 * conv_w + ssm_w]
    cols = lambda a, b: w_in[:, :, a:b].astype(BF16)
    kpe = cols(o[2], o[3])
    qk = NOPE_DIM + ROPE_DIM
    uq = w_uq.astype(BF16).reshape(ly, q_lora, N_HEADS, qk)
    pe = uq[..., NOPE_DIM:]
    ukv = w_ukv.astype(BF16).reshape(ly, kv_lora, N_HEADS, NOPE_DIM + V_DIM)
    return dict(
        w_mod=w_mod, b_mod=b_mod[:, None, :],
        w_a=jnp.concatenate([cols(0, o[2]), kpe, kpe[:, :, swap]], axis=2),
        w_branch=cols(o[3], w_in.shape[2]), branch_cols=[c - o[3] for c in o[3:]],
        w_uq=jnp.concatenate([uq[..., :NOPE_DIM], pe, pe[..., swap]], axis=-1).reshape(ly, q_lora, N_HEADS * HEAD_W),
        w_uk=ukv[..., :NOPE_DIM].reshape(ly, kv_lora, N_HEADS * NOPE_DIM),
        w_uv=ukv[..., NOPE_DIM:].reshape(ly, kv_lora, N_HEADS * V_DIM),
        w_attn_o=w_attn_o.astype(BF16), w_conv_o=w_conv_o.astype(BF16), w_glu=w_glu.astype(BF16),
        w_ssm_o=w_ssm_o.astype(BF16), w_out=w_out.astype(BF16), w_ff1=w_ff1.astype(BF16),
        w_ff2=w_ff2.astype(BF16),
    )


def _block(x, mod, pp, vec, ssm_m, layer, batch, seq_len, rope_tabs, cache):
    shift1, scale1, gate1, shift2, scale2, gate2 = mod
    t = x.shape[0]
    cc, ss = rope_tabs
    w_br = pp["w_branch"]
    c_conv_a, c_conv_b, c_ssm, c_gate = pp["branch_cols"]
    n_gate = w_br.shape[2] - c_gate

    q, ckv, kpe2, h = _attn_in(x, vec["g_mix_pre"], scale1, shift1, pp["w_a"], vec["g_q"], vec["g_kv"],
                               pp["w_uq"], cc, ss, layer, seq_len)
    k_new, v_new = _kv_build(ckv, kpe2, cc, ss, pp["w_uk"], pp["w_uv"], layer)
    shape3 = lambda a, b: a.reshape(b, a.shape[0] // b, a.shape[1])
    kvs = [(shape3(k_new, batch), shape3(v_new, batch))]
    if cache is not None:
        ckv_ctx, kpe_ctx, icc, iss = cache
        k_ctx, v_ctx = _kv_build(ckv_ctx, kpe_ctx, icc, iss, pp["w_uk"], pp["w_uv"], layer)
        kvs.append((shape3(k_ctx, batch), shape3(v_ctx, batch)))
    heads_per_step = min(N_HEADS, 8 if seq_len <= 256 else 4)
    ao = _attention(shape3(q, batch), kvs, heads_per_step).reshape(t, N_HEADS * V_DIM)

    z = _proj(h, w_br, [c_conv_a, c_conv_b], c_conv_b - c_conv_a, layer, "glu")
    cv = _conv_branch(z, vec["conv_w"], vec["conv_b"], layer, batch, seq_len)

    u = _proj(h, w_br, [c_ssm], c_gate - c_ssm, layer, "none", out_dtype=F32)
    y, fin = _ssm_scan(u, ssm_m["h0"], ssm_m["t"], ssm_m["p"], ssm_m["rf"], ssm_m["rb"], ssm_m["a"], layer, batch)
    sa = _ssm_post(y, pp["w_glu"], vec["b_glu"], layer)

    gates = _proj(h, w_br, [c_gate], n_gate, layer, "sigmoid")
    x = _branch_out(ao, cv, sa, gates, x, pp["w_attn_o"], pp["w_conv_o"], pp["w_ssm_o"], pp["w_out"],
                    vec["conv_ln_g"], vec["conv_ln_b"], vec["g_mix_post"], gate1, layer, seq_len)
    x = _mlp(x, vec["g_mlp_pre"], scale2, shift2, pp["w_ff1"], pp["w_ff2"], vec["g_mlp_post"], gate2, layer,
             seq_len)
    return x, ckv, kpe2[:, :ROPE_DIM], fin


def kernel(x_prompt, x_sample, cache_ckv, cache_kpe, state_ssm_re, state_ssm_im, c, c_ctx, w_mod, b_mod, g_mix_pre, g_mix_post, g_mlp_pre, g_mlp_post, w_in, g_q, w_uq, g_kv, w_ukv, w_attn_o, conv_w, conv_b, conv_ln_g, conv_ln_b, w_conv_o, ssm_a_re, ssm_a_im, ssm_log_dt, ssm_b_re, ssm_b_im, ssm_c_re, ssm_c_im, ssm_d, w_glu, b_glu, w_ssm_o, w_out, w_ff1, w_ff2):
    bp, lp, d = x_prompt.shape
    bs, ls, _ = x_sample.shape
    depth = w_in.shape[0]
    past = cache_ckv.shape[2]
    q_lora, kv_lora = g_q.shape[1], g_kv.shape[1]
    conv_ch, ssm_ch = conv_w.shape[2], ssm_d.shape[1]
    n_groups = ssm_ch // SSM_GROUP_CH

    pp = _prepare_params(w_mod, b_mod, w_in, w_uq, w_ukv, w_attn_o, w_conv_o, w_glu, w_ssm_o, w_out, w_ff1,
                         w_ff2, q_lora, kv_lora, conv_ch, ssm_ch)
    row = lambda a: a[:, None, :]
    vec = dict(g_mix_pre=row(g_mix_pre), g_mix_post=row(g_mix_post), g_mlp_pre=row(g_mlp_pre),
               g_mlp_post=row(g_mlp_post), g_q=row(g_q), g_kv=row(g_kv), conv_w=conv_w, conv_b=row(conv_b),
               conv_ln_g=row(conv_ln_g), conv_ln_b=row(conv_ln_b), b_glu=row(b_glu))
    t_m, p_m, rf_m, rb_m, a_m = _ssm_matrices(ssm_a_re, ssm_a_im, ssm_log_dt, ssm_b_re, ssm_b_im, ssm_c_re, ssm_c_im,
                                       ssm_d)

    n_cond = 1 + bs
    cond = jnp.concatenate([c_ctx[None, :], c, jnp.zeros((-n_cond % 8, d), F32)], axis=0)
    mod_all = _modulation(cond, pp["w_mod"], pp["b_mod"])

    rope_s = _rope_tables(ls)
    ident_p = _identity_tables(min(lp, 256))
    ident_c = _identity_tables(min(past, 256))
    zero_h0 = jnp.zeros((n_groups, bp, 4 * SSM_STATE), F32)

    xp = x_prompt.reshape(bp * lp, d)
    xs = x_sample.reshape(bs * ls, d)
    ckvs, kpes, fins = [], [], []
    for l in range(depth):
        mods = jnp.split(mod_all[l], 6, axis=-1)
        mod_ctx = [m[0:1, None, :] for m in mods]
        mod_lat = [m[1:n_cond, None, :] for m in mods]
        ssm_p = dict(t=t_m, p=p_m, rf=rf_m, rb=rb_m, a=a_m, h0=zero_h0)
        xp, ckv, kpe, fin = _block(xp, mod_ctx, pp, vec, ssm_p, l, bp, lp, ident_p, None)
        ckvs.append(ckv.reshape(bp, lp, kv_lora))
        kpes.append(kpe.reshape(bp, lp, ROPE_DIM))
        fins.append(fin)
        sr, si = state_ssm_re[:, l], state_ssm_im[:, l]
        h0 = jnp.concatenate([sr[:, 0], sr[:, 1], si[:, 0], si[:, 1]], axis=-1).transpose(1, 0, 2)
        kpe_ctx = jnp.pad(cache_kpe[:, l].reshape(bs * past, ROPE_DIM), ((0, 0), (0, ROPE_DIM)))
        cache = (cache_ckv[:, l].reshape(bs * past, kv_lora), kpe_ctx) + ident_c
        ssm_s = dict(t=t_m, p=p_m, rf=rf_m, rb=rb_m, a=a_m, h0=h0)
        xs = _block(xs, mod_lat, pp, vec, ssm_s, l, bs, ls, rope_s, cache)[0]

    new_ckv = jnp.stack(ckvs, axis=1)
    new_kpe = jnp.stack(kpes, axis=1)
    fin = jnp.stack(fins, axis=0)
    fin = fin.reshape(depth, n_groups, bp, 2, 2, SSM_STATE).transpose(3, 2, 0, 4, 1, 5)
    return (xp.reshape(bp, lp, d), xs.reshape(bs, ls, d), new_ckv, new_kpe, fin[0], fin[1])
```

```python
import functools
import math

import jax
import jax.numpy as jnp
from jax import lax
from jax.experimental import pallas as pl
from jax.experimental.pallas import tpu as pltpu

N_HEADS = 16
NOPE_DIM = 128
ROPE_DIM = 64
V_DIM = 128
GRID_W = 64
ROPE_BASE = 10000.0
SSM_GROUP_CH = 16
SSM_STATE = 64
EPS = 1e-6

HEAD_W = NOPE_DIM + 2 * ROPE_DIM
_Q_SCALE = math.log2(math.e) / math.sqrt(NOPE_DIM + ROPE_DIM)
SSM_CHUNK = 16
V7X_VMEM_LIMIT_BYTES = 56 * 1024 * 1024
_LANES = 128

F32 = jnp.float32
BF16 = jnp.bfloat16


def _cparams(*sem):
    return pltpu.CompilerParams(dimension_semantics=sem, vmem_limit_bytes=V7X_VMEM_LIMIT_BYTES)


def _tile(n, target):
    if n <= target:
        return n
    for t in range(target, 7, -1):
        if n % t == 0 and t % 8 == 0:
            return t
    return n


def _rms(x, g):
    return x * lax.rsqrt(jnp.mean(x * x, axis=-1, keepdims=True) + EPS) * g


def _rmsmod(x, g, scale, shift):
    return _rms(x, g) * (1.0 + scale) + shift


def _mod_kernel(c_ref, w_ref, b_ref, o_ref):
    h = jax.nn.silu(c_ref[...]).astype(BF16)
    o_ref[...] = jnp.dot(h, w_ref[...].astype(BF16), preferred_element_type=F32) + b_ref[...]


def _modulation(cond, w_mod, b_mod):
    ly, d, n = w_mod.shape
    r = cond.shape[0]
    tn = _tile(n, 1536)
    return pl.pallas_call(
        _mod_kernel,
        grid=(ly, n // tn),
        in_specs=[
            pl.BlockSpec((r, d), lambda l, j: (0, 0)),
            pl.BlockSpec((None, d, tn), lambda l, j: (l, 0, j)),
            pl.BlockSpec((None, 1, tn), lambda l, j: (l, 0, j)),
        ],
        out_specs=pl.BlockSpec((None, r, tn), lambda l, j: (l, 0, j)),
        out_shape=jax.ShapeDtypeStruct((ly, r, n), F32),
        compiler_params=_cparams("parallel", "parallel"),
        name="modulation",
    )(cond, w_mod, b_mod)


def _proj_kernel(h_ref, *rest, n_w, epilogue):
    w_refs, o_ref = rest[:n_w], rest[n_w]
    h = h_ref[...]
    a = jnp.dot(h, w_refs[0][...], preferred_element_type=F32)
    if epilogue == "glu":
        a = a * jax.nn.sigmoid(jnp.dot(h, w_refs[1][...], preferred_element_type=F32))
    elif epilogue == "sigmoid":
        a = jax.nn.sigmoid(a)
    o_ref[...] = a.astype(o_ref.dtype)


def _proj(h, w, col_starts, n, layer, epilogue, out_dtype=BF16, tm=1024, tn=1024):
    t, d = h.shape
    tm = _tile(t, tm)
    tn = next(c for c in range(min(n, tn // len(col_starts)), 0, -_LANES)
              if n % c == 0 and all(s % c == 0 for s in col_starts))
    w_specs = [pl.BlockSpec((None, d, tn), functools.partial(lambda i, j, off: (layer, 0, off + j), off=c // tn))
               for c in col_starts]
    return pl.pallas_call(
        functools.partial(_proj_kernel, n_w=len(col_starts), epilogue=epilogue),
        grid=(t // tm, n // tn),
        in_specs=[pl.BlockSpec((tm, d), lambda i, j: (i, 0))] + w_specs,
        out_specs=pl.BlockSpec((tm, tn), lambda i, j: (i, j)),
        out_shape=jax.ShapeDtypeStruct((t, n), out_dtype),
        compiler_params=_cparams("parallel", "parallel"),
        name="proj_" + epilogue,
    )(h, *([w] * len(col_starts)))


def _attn_in_kernel(x_ref, g_ref, sc_ref, sh_ref, wa_ref, gq_ref, gkv_ref, wuq_ref, cc_ref, ss_ref,
                    q_ref, ckv_ref, kpe_ref, h_ref, *, q_lora, kv_lora):
    h = _rmsmod(x_ref[...], g_ref[...], sc_ref[0], sh_ref[0]).astype(BF16)
    h_ref[...] = h
    a = jnp.dot(h, wa_ref[...], preferred_element_type=F32)
    cq = _rms(a[:, :q_lora], gq_ref[...]).astype(BF16)
    ckv_ref[...] = _rms(a[:, q_lora:q_lora + kv_lora], gkv_ref[...])
    kpe_ref[...] = a[:, q_lora + kv_lora:]
    q = jnp.dot(cq, wuq_ref[...], preferred_element_type=F32) * _Q_SCALE
    cc, ss = cc_ref[...], ss_ref[...]
    for hd in range(N_HEADS):
        lo = hd * HEAD_W
        q_ref[:, lo:lo + NOPE_DIM] = q[:, lo:lo + NOPE_DIM].astype(BF16)
        v = q[:, lo + NOPE_DIM:lo + HEAD_W]
        q_ref[:, lo + NOPE_DIM:lo + HEAD_W] = (v * cc + pltpu.roll(v, ROPE_DIM, 1) * ss).astype(BF16)


def _attn_in(x, g, scale, shift, w_a, g_q, g_kv, w_uq, cc, ss, layer, seq_len, tm=256):
    t, d = x.shape
    na = w_a.shape[2]
    q_lora, kv_lora = g_q.shape[2], g_kv.shape[2]
    nq = w_uq.shape[2]
    tm = _tile(seq_len, tm)
    per_seq = seq_len // tm
    nb = scale.shape[0]
    bidx = (lambda i: i // per_seq) if nb > 1 else (lambda i: 0)
    ntab = cc.shape[0] // tm
    lsel = lambda i: (layer, 0, 0)
    return pl.pallas_call(
        functools.partial(_attn_in_kernel, q_lora=q_lora, kv_lora=kv_lora),
        grid=(t // tm,),
        in_specs=[
            pl.BlockSpec((tm, d), lambda i: (i, 0)),
            pl.BlockSpec((None, 1, d), lsel),
            pl.BlockSpec((1, 1, d), lambda i: (bidx(i), 0, 0)),
            pl.BlockSpec((1, 1, d), lambda i: (bidx(i), 0, 0)),
            pl.BlockSpec((None, d, na), lsel),
            pl.BlockSpec((None, 1, q_lora), lsel),
            pl.BlockSpec((None, 1, kv_lora), lsel),
            pl.BlockSpec((None, q_lora, nq), lsel),
            pl.BlockSpec((tm, 2 * ROPE_DIM), lambda i: (i % ntab, 0)),
            pl.BlockSpec((tm, 2 * ROPE_DIM), lambda i: (i % ntab, 0)),
        ],
        out_specs=[
            pl.BlockSpec((tm, nq), lambda i: (i, 0)),
            pl.BlockSpec((tm, kv_lora), lambda i: (i, 0)),
            pl.BlockSpec((tm, 2 * ROPE_DIM), lambda i: (i, 0)),
            pl.BlockSpec((tm, d), lambda i: (i, 0)),
        ],
        out_shape=[
            jax.ShapeDtypeStruct((t, nq), BF16),
            jax.ShapeDtypeStruct((t, kv_lora), F32),
            jax.ShapeDtypeStruct((t, 2 * ROPE_DIM), F32),
            jax.ShapeDtypeStruct((t, d), BF16),
        ],
        compiler_params=_cparams("parallel"),
        name="attn_in",
    )(x, g, scale, shift, w_a, g_q, g_kv, w_uq, cc, ss)


_NT_DIMS = (((1,), (1,)), ((), ()))


def _kv_build_kernel(ckv_ref, kpe_ref, cc_ref, ss_ref, wuk_ref, wuv_ref, k_ref, v_ref):
    c = ckv_ref[...].astype(BF16)
    kn = jnp.dot(c, wuk_ref[...], preferred_element_type=F32)
    v_ref[...] = jnp.dot(c, wuv_ref[...], preferred_element_type=F32).astype(BF16)
    kp = kpe_ref[...]
    kp = (kp * cc_ref[...] + pltpu.roll(kp, ROPE_DIM, 1) * ss_ref[...]).astype(BF16)
    for hd in range(N_HEADS):
        k_ref[:, hd * HEAD_W:hd * HEAD_W + NOPE_DIM] = kn[:, hd * NOPE_DIM:(hd + 1) * NOPE_DIM].astype(BF16)
        k_ref[:, hd * HEAD_W + NOPE_DIM:(hd + 1) * HEAD_W] = kp


def _kv_build(ckv, kpe2, cc, ss, w_uk, w_uv, layer, tm=256):
    t, kvl = ckv.shape
    tm = _tile(cc.shape[0], tm)
    ntab = cc.shape[0] // tm
    nk, nv = N_HEADS * HEAD_W, w_uv.shape[2]
    lsel = lambda i: (layer, 0, 0)
    return pl.pallas_call(
        _kv_build_kernel,
        grid=(t // tm,),
        in_specs=[
            pl.BlockSpec((tm, kvl), lambda i: (i, 0)),
            pl.BlockSpec((tm, 2 * ROPE_DIM), lambda i: (i, 0)),
            pl.BlockSpec((tm, 2 * ROPE_DIM), lambda i: (i % ntab, 0)),
            pl.BlockSpec((tm, 2 * ROPE_DIM), lambda i: (i % ntab, 0)),
            pl.BlockSpec((None, kvl, w_uk.shape[2]), lsel),
            pl.BlockSpec((None, kvl, nv), lsel),
        ],
        out_specs=[pl.BlockSpec((tm, nk), lambda i: (i, 0)), pl.BlockSpec((tm, nv), lambda i: (i, 0))],
        out_shape=[jax.ShapeDtypeStruct((t, nk), BF16), jax.ShapeDtypeStruct((t, nv), BF16)],
        compiler_params=_cparams("parallel"),
        name="kv_build",
    )(ckv, kpe2, cc, ss, w_uk, w_uv)


def _attn_kernel(q_ref, *rest, n_src, heads):
    kv_refs, o_ref = rest[:2 * n_src], rest[2 * n_src]
    for hd in range(heads):
        q = q_ref[:, hd * HEAD_W:(hd + 1) * HEAD_W]
        s = [lax.dot_general(q, kv_refs[2 * i][:, hd * HEAD_W:(hd + 1) * HEAD_W], _NT_DIMS,
                             preferred_element_type=F32) for i in range(n_src)]
        m = jnp.max(s[0], axis=-1, keepdims=True)
        for si in s[1:]:
            m = jnp.maximum(m, jnp.max(si, axis=-1, keepdims=True))
        acc, den = None, None
        for i in range(n_src):
            p = jnp.exp2(s[i] - m)
            li = jnp.sum(p, axis=-1, keepdims=True)
            oi = jnp.dot(p.astype(BF16), kv_refs[2 * i + 1][:, hd * V_DIM:(hd + 1) * V_DIM],
                         preferred_element_type=F32)
            acc = oi if acc is None else acc + oi
            den = li if den is None else den + li
        o_ref[:, hd * V_DIM:(hd + 1) * V_DIM] = (acc / den).astype(BF16)


def _attention(q, kvs, heads_per_step, tq=256):
    b, sq, _ = q.shape
    tq = _tile(sq, tq)
    hb = heads_per_step
    in_specs = [pl.BlockSpec((None, tq, hb * HEAD_W), lambda bi, hi, qi: (bi, qi, hi))]
    args = [q]
    for k, v in kvs:
        sk = k.shape[1]
        in_specs.append(pl.BlockSpec((None, sk, hb * HEAD_W), lambda bi, hi, qi: (bi, 0, hi)))
        in_specs.append(pl.BlockSpec((None, sk, hb * V_DIM), lambda bi, hi, qi: (bi, 0, hi)))
        args += [k, v]
    return pl.pallas_call(
        functools.partial(_attn_kernel, n_src=len(kvs), heads=hb),
        grid=(b, N_HEADS // hb, sq // tq),
        in_specs=in_specs,
        out_specs=pl.BlockSpec((None, tq, hb * V_DIM), lambda bi, hi, qi: (bi, qi, hi)),
        out_shape=jax.ShapeDtypeStruct((b, sq, N_HEADS * V_DIM), BF16),
        compiler_params=_cparams("parallel", "parallel", "parallel"),
        name="attention",
    )(*args)


_CONV_ROWS = 64
_CONV_LANES = 256
_CONV_HALO = 16
_SUBLANES = 8


def _conv_kernel(z_ref, w_ref, b_ref, o_ref, zs_ref, wb_ref, *, seq_len, taps):
    lanes = z_ref.shape[-1]
    pad = (taps - 1) // 2
    halo = jnp.zeros((_CONV_HALO, lanes), F32)
    zp = jnp.concatenate([halo, z_ref[...].astype(F32), halo], axis=0)
    span = seq_len + 2 * _CONV_HALO - _SUBLANES
    for j in range(_SUBLANES):
        zs_ref[j, 0:span, :] = zp[j:j + span, :]
    for k in range(taps):
        wb_ref[k] = jnp.broadcast_to(w_ref[k:k + 1, :], (_SUBLANES, lanes))
    bias = jnp.broadcast_to(b_ref[...], (_SUBLANES, lanes))
    n_sub = _CONV_ROWS // _SUBLANES

    def conv_step(it, carry):
        r0 = pl.multiple_of(it * _CONV_ROWS, _CONV_ROWS)
        acc = [bias] * n_sub
        for k in range(taps):
            off = _CONV_HALO - pad + k
            j, base = off % _SUBLANES, off - off % _SUBLANES
            win = zs_ref[j, pl.ds(r0 + base, _CONV_ROWS), :]
            wk = wb_ref[k]
            acc = [acc[p] + wk * win[p * _SUBLANES:(p + 1) * _SUBLANES, :] for p in range(n_sub)]
        o_ref[pl.ds(r0, _CONV_ROWS), :] = jnp.concatenate(acc, axis=0).astype(BF16)
        return carry

    lax.fori_loop(0, seq_len // _CONV_ROWS, conv_step, 0)


def _conv_branch(z, conv_w, conv_b, layer, batch, seq_len):
    t, ch = z.shape
    taps = conv_w.shape[1]
    lanes = _tile(ch, _CONV_LANES)
    return pl.pallas_call(
        functools.partial(_conv_kernel, seq_len=seq_len, taps=taps),
        grid=(batch, ch // lanes),
        in_specs=[
            pl.BlockSpec((seq_len, lanes), lambda b, c: (b, c)),
            pl.BlockSpec((None, taps, lanes), lambda b, c: (layer, 0, c)),
            pl.BlockSpec((None, 1, lanes), lambda b, c: (layer, 0, c)),
        ],
        out_specs=pl.BlockSpec((seq_len, lanes), lambda b, c: (b, c)),
        out_shape=jax.ShapeDtypeStruct((t, ch), BF16),
        scratch_shapes=[pltpu.VMEM((_SUBLANES, seq_len + 2 * _CONV_HALO, lanes), F32),
                        pltpu.VMEM((taps, _SUBLANES, lanes), F32)],
        compiler_params=_cparams("parallel", "parallel"),
        name="depthwise_conv",
    )(z, conv_w, conv_b)


_SSM_TILE_GROUPS = _LANES // SSM_GROUP_CH
_SSM_MOVE_ROWS = 64


def _ssm_kernel(x_ref, h0_ref, t_ref, p_ref, rf_ref, rb_ref, a_ref, y_ref, fin_ref,
                u_scr, y_scr, ds_scr, sf_scr, sb_scr, *, n_chunks, batch):
    m = n_chunks * batch
    ng, q, gc = _SSM_TILE_GROUPS, SSM_CHUNK, SSM_GROUP_CH
    half = 2 * SSM_STATE
    rr = min(_SSM_MOVE_ROWS, m)
    blk = lax.broadcasted_iota(jnp.int32, (rr, _LANES), 1) // gc
    is_fwd = lax.broadcasted_iota(jnp.int32, (batch, half), 1) < SSM_STATE

    def block_transpose(vs):
        vs = list(vs)
        d = ng // 2
        while d >= 1:
            odd = (blk // d) % 2 == 1
            nxt = list(vs)
            for a in range(ng):
                if (a // d) % 2 == 0:
                    lo, hi = vs[a], vs[a + d]
                    nxt[a] = jnp.where(odd, pltpu.roll(hi, d * gc, 1), lo)
                    nxt[a + d] = jnp.where(odd, hi, pltpu.roll(lo, _LANES - d * gc, 1))
            vs = nxt
            d //= 2
        return vs

    def gather_in(it, carry):
        r0 = pl.multiple_of(it * rr, rr)
        halves = [block_transpose([x_ref[pl.ds(r0, rr), h * ng + j, :] for j in range(ng)])
                  for h in range(q // ng)]
        for g in range(ng):
            u_scr[g, pl.ds(r0, rr), :] = jnp.concatenate([hv[g] for hv in halves], axis=1).astype(BF16)
        return carry

    lax.fori_loop(0, m // rr, gather_in, 0)

    def one_group(g, carry):
        u = u_scr[g]
        ds_scr[...] = jnp.dot(u, p_ref[g], preferred_element_type=F32).reshape(batch, n_chunks, 2 * half)
        a = a_ref[g]
        h0 = h0_ref[g]
        a_re, a_im = a[:, :half], a[:, half:]
        s_re, s_im = h0[:, :half], h0[:, half:]
        for c in range(n_chunks):
            cb = n_chunks - 1 - c
            state = jnp.concatenate([s_re, s_im], axis=1)
            sf_scr[c] = state
            sb_scr[cb] = state
            d_re = jnp.where(is_fwd, ds_scr[:, c, :half], ds_scr[:, cb, :half])
            d_im = jnp.where(is_fwd, ds_scr[:, c, half:], ds_scr[:, cb, half:])
            s_re, s_im = (a_re * s_re - a_im * s_im + d_re, a_re * s_im + a_im * s_re + d_im)
        fin_ref[g] = jnp.concatenate([s_re, s_im], axis=1)
        by_batch = lambda scr: jnp.concatenate([scr[:, b, :] for b in range(batch)], axis=0).astype(BF16)
        y = jnp.dot(u, t_ref[g], preferred_element_type=F32)
        y = y + jnp.dot(by_batch(sf_scr), rf_ref[g], preferred_element_type=F32)
        y = y + jnp.dot(by_batch(sb_scr), rb_ref[g], preferred_element_type=F32)
        y_scr[g] = y
        return carry

    lax.fori_loop(0, ng, one_group, 0)

    def scatter_out(it, carry):
        r0 = pl.multiple_of(it * rr, rr)
        ys = [y_scr[g, pl.ds(r0, rr), :] for g in range(ng)]
        for h in range(q // ng):
            toks = block_transpose([y[:, h * _LANES:(h + 1) * _LANES] for y in ys])
            for j in range(ng):
                y_ref[pl.ds(r0, rr), h * ng + j, :] = toks[j]
        return carry

    lax.fori_loop(0, m // rr, scatter_out, 0)


def _ssm_scan(u, h0, t_m, p_m, rf_m, rb_m, a_m, layer, batch):
    t, ch = u.shape
    m = t // SSM_CHUNK
    n_chunks = m // batch
    ng, w = _SSM_TILE_GROUPS, SSM_CHUNK * SSM_GROUP_CH
    g_all = ch // SSM_GROUP_CH
    tok = pl.BlockSpec((m, SSM_CHUNK, _LANES), lambda i: (0, 0, i))
    wsel = lambda i: (layer, i, 0, 0)
    mat = pl.BlockSpec((None, ng, w, w), wsel)
    st = pl.BlockSpec((ng, batch, w), lambda i: (i, 0, 0))
    y, fin = pl.pallas_call(
        functools.partial(_ssm_kernel, n_chunks=n_chunks, batch=batch),
        grid=(g_all // ng,),
        in_specs=[tok, st, mat, mat, mat, mat, pl.BlockSpec((None, ng, 1, w), wsel)],
        out_specs=[tok, st],
        out_shape=[jax.ShapeDtypeStruct((m, SSM_CHUNK, ch), F32), jax.ShapeDtypeStruct((g_all, batch, w), F32)],
        scratch_shapes=[pltpu.VMEM((ng, m, w), BF16), pltpu.VMEM((ng, m, w), F32),
                        pltpu.VMEM((batch, n_chunks, w), F32), pltpu.VMEM((n_chunks, batch, w), F32),
                        pltpu.VMEM((n_chunks, batch, w), F32)],
        compiler_params=_cparams("parallel"),
        name="ssm_chunk_scan",
    )(u.reshape(m, SSM_CHUNK, ch), h0, t_m, p_m, rf_m, rb_m, a_m)
    return y.reshape(t, ch), fin


def _ssm_post_kernel(y_ref, w_ref, b_ref, o_ref):
    g = jax.nn.gelu(y_ref[...].astype(F32))
    s = jnp.dot(g.astype(BF16), w_ref[...], preferred_element_type=F32) + b_ref[...]
    o_ref[...] = (g * jax.nn.sigmoid(s)).astype(BF16)


def _ssm_post(y, w_glu, b_glu, layer, tm=512):
    t, w = y.shape
    tm = _tile(t, tm)
    lsel = lambda i: (layer, 0, 0)
    return pl.pallas_call(
        _ssm_post_kernel,
        grid=(t // tm,),
        in_specs=[pl.BlockSpec((tm, w), lambda i: (i, 0)), pl.BlockSpec((None, w, w), lsel),
                  pl.BlockSpec((None, 1, w), lsel)],
        out_specs=pl.BlockSpec((tm, w), lambda i: (i, 0)),
        out_shape=jax.ShapeDtypeStruct((t, w), BF16),
        compiler_params=_cparams("parallel"),
        name="ssm_gelu_glu",
    )(y, w_glu, b_glu)


def _accumulate_then(acc_ref, part, k, nk, finish):
    if nk == 1:
        finish(part)
        return

    @pl.when(k == 0)
    def _():
        acc_ref[...] = part

    @pl.when(jnp.logical_and(k > 0, k < nk - 1))
    def _():
        acc_ref[...] += part

    @pl.when(k == nk - 1)
    def _():
        finish(acc_ref[...] + part)


def _branch_out_kernel(ao_ref, cv_ref, sa_ref, ga_ref, gc_ref, gs_ref, wa_ref, wc_ref, ws_ref, wo_ref,
                       lg_ref, lb_ref, g_ref, gate_ref, x_ref, o_ref, ca_ref, acc_ref, *, nj):
    j = pl.program_id(1)

    @pl.when(j == 0)
    def _():
        v = cv_ref[...].astype(F32)
        vc = v - jnp.mean(v, axis=-1, keepdims=True)
        var = jnp.mean(vc * vc, axis=-1, keepdims=True)
        y = vc * lax.rsqrt(var + EPS) * lg_ref[...] + lb_ref[...]
        ca_ref[...] = jax.nn.silu(y).astype(BF16)

    a = jnp.dot(ao_ref[...], wa_ref[...], preferred_element_type=F32)
    c = jnp.dot(ca_ref[...], wc_ref[...], preferred_element_type=F32)
    s = jnp.dot(sa_ref[...], ws_ref[...], preferred_element_type=F32)
    mix = ga_ref[...].astype(F32) * a + gc_ref[...].astype(F32) * c + gs_ref[...].astype(F32) * s
    part = jnp.dot(mix.astype(BF16), wo_ref[...], preferred_element_type=F32)

    def finish(total):
        o_ref[...] = x_ref[...] + gate_ref[0] * _rms(total, g_ref[...])

    _accumulate_then(acc_ref, part, j, nj, finish)


def _branch_out(ao, cv, sa, gates, x, w_attn_o, w_conv_o, w_ssm_o, w_out, ln_g, ln_b, g_post, gate, layer,
                seq_len, tm=512, tn=512):
    t, d = x.shape
    nb = gate.shape[0]
    tm, tn = _tile(t if nb == 1 else seq_len, tm), _tile(d, tn)
    nj = d // tn
    per_seq = seq_len // tm
    bidx = (lambda i: i // per_seq) if nb > 1 else (lambda i: 0)
    act = lambda a: pl.BlockSpec((tm, a.shape[1]), lambda i, j: (i, 0))
    wcol = lambda w: pl.BlockSpec((None, w.shape[1], tn), lambda i, j: (layer, 0, j))
    gcol = lambda k: pl.BlockSpec((tm, tn), lambda i, j: (i, k * nj + j))
    vec = lambda v: pl.BlockSpec((None, 1, v.shape[2]), lambda i, j: (layer, 0, 0))
    return pl.pallas_call(
        functools.partial(_branch_out_kernel, nj=nj),
        grid=(t // tm, nj),
        in_specs=[act(ao), act(cv), act(sa), gcol(0), gcol(1), gcol(2),
                  wcol(w_attn_o), wcol(w_conv_o), wcol(w_ssm_o),
                  pl.BlockSpec((None, tn, d), lambda i, j: (layer, j, 0)),
                  vec(ln_g), vec(ln_b), vec(g_post),
                  pl.BlockSpec((1, 1, d), lambda i, j: (bidx(i), 0, 0)),
                  pl.BlockSpec((tm, d), lambda i, j: (i, 0))],
        out_specs=pl.BlockSpec((tm, d), lambda i, j: (i, 0)),
        out_shape=jax.ShapeDtypeStruct((t, d), F32),
        scratch_shapes=[pltpu.VMEM((tm, cv.shape[1]), BF16), pltpu.VMEM((tm, d), F32)],
        compiler_params=_cparams("parallel", "arbitrary"),
        name="branch_out",
    )(ao, cv, sa, gates, gates, gates, w_attn_o, w_conv_o, w_ssm_o, w_out, ln_g, ln_b, g_post, gate, x)


def _mlp_kernel(x_ref, gpre_ref, sc_ref, sh_ref, w1_ref, w2_ref, gpost_ref, gate_ref, o_ref, h_ref, acc_ref, *,
                nk):
    k = pl.program_id(1)

    @pl.when(k == 0)
    def _():
        h_ref[...] = _rmsmod(x_ref[...], gpre_ref[...], sc_ref[0], sh_ref[0]).astype(BF16)

    f = jnp.square(jnp.maximum(jnp.dot(h_ref[...], w1_ref[...], preferred_element_type=F32), 0.0))
    part = jnp.dot(f.astype(BF16), w2_ref[...], preferred_element_type=F32)

    def finish(total):
        o_ref[...] = x_ref[...] + gate_ref[0] * _rms(total, gpost_ref[...])

    _accumulate_then(acc_ref, part, k, nk, finish)


def _mlp(x, g_pre, scale, shift, w1, w2, g_post, gate, layer, seq_len, tm=512, tf=1024):
    t, d = x.shape
    ff = w1.shape[2]
    nb = gate.shape[0]
    tm, tf = _tile(t if nb == 1 else seq_len, tm), _tile(ff, tf)
    nk = ff // tf
    per_seq = seq_len // tm
    bidx = (lambda i: i // per_seq) if nb > 1 else (lambda i: 0)
    vec = pl.BlockSpec((None, 1, d), lambda i, k: (layer, 0, 0))
    mod = pl.BlockSpec((1, 1, d), lambda i, k: (bidx(i), 0, 0))
    return pl.pallas_call(
        functools.partial(_mlp_kernel, nk=nk),
        grid=(t // tm, nk),
        in_specs=[pl.BlockSpec((tm, d), lambda i, k: (i, 0)), vec, mod, mod,
                  pl.BlockSpec((None, d, tf), lambda i, k: (layer, 0, k)),
                  pl.BlockSpec((None, tf, d), lambda i, k: (layer, k, 0)),
                  vec, mod],
        out_specs=pl.BlockSpec((tm, d), lambda i, k: (i, 0)),
        out_shape=jax.ShapeDtypeStruct((t, d), F32),
        scratch_shapes=[pltpu.VMEM((tm, d), BF16), pltpu.VMEM((tm, d), F32)],
        compiler_params=_cparams("parallel", "arbitrary"),
        name="mlp_relu2",
    )(x, g_pre, scale, shift, w1, w2, g_post, gate)


def _rope_swap_index():
    q = ROPE_DIM // 4
    return jnp.concatenate([jnp.arange(q, 2 * q), jnp.arange(0, q), jnp.arange(3 * q, 4 * q), jnp.arange(2 * q, 3 * q)])


def _rope_tables(n_tok):
    pairs = ROPE_DIM // 4
    pos = jnp.arange(n_tok)
    row = (pos // GRID_W).astype(F32)
    col = (pos % GRID_W).astype(F32)
    inv = ROPE_BASE ** (-jnp.arange(pairs, dtype=F32) / pairs)
    ang = jnp.stack([row[:, None] * inv, col[:, None] * inv], axis=1)
    cos, sin = jnp.cos(ang), jnp.sin(ang)
    cc = jnp.stack([cos, cos], axis=2).reshape(n_tok, ROPE_DIM)
    ss = jnp.stack([-sin, sin], axis=2).reshape(n_tok, ROPE_DIM)
    z = jnp.zeros((n_tok, ROPE_DIM), F32)
    return jnp.concatenate([cc, z], axis=1), jnp.concatenate([ss, z], axis=1)


def _identity_tables(n_tok):
    one = jnp.ones((n_tok, ROPE_DIM), F32)
    z = jnp.zeros((n_tok, ROPE_DIM), F32)
    return jnp.concatenate([one, z], axis=1), jnp.zeros((n_tok, 2 * ROPE_DIM), F32)


_SSM_MAT_GROUPS = 8


def _ssm_mats_kernel(a1_ref, a2_ref, ai_ref, ldt_ref, bx_ref, by_ref, cx_ref, cy_ref, d_ref,
                     t_ref, p_ref, rf_ref, rb_ref, lam_ref, *, groups):
    q, gc, ns = SSM_CHUNK, SSM_GROUP_CH, SSM_STATE
    w = q * gc
    hp = lax.Precision.HIGHEST
    n_pow = 24
    tau = lax.broadcasted_iota(jnp.int32, (n_pow, 2 * ns), 0).astype(F32)
    first = lax.broadcasted_iota(jnp.int32, (1, 2 * ns), 1) < ns
    quarter = jnp.where(first, 0.0, 0.5 * math.pi)
    one_zero = jnp.where(first, 1.0, 0.0)
    lane_w = lax.broadcasted_iota(jnp.int32, (gc, w), 1)
    eye = (lax.broadcasted_iota(jnp.int32, (gc, gc), 0) == lax.broadcasted_iota(jnp.int32, (gc, gc), 1))
    swap = lambda x: pltpu.roll(x, ns, 1)
    nt_dot = lambda x, y: lax.dot_general(x, y, (((1,), (1,)), ((), ())), precision=hp,
                                          preferred_element_type=F32)

    def direction(d, g):
        dt = jnp.exp(ldt_ref[d, g])
        a1, a2 = a1_ref[d, g], a2_ref[d, g]
        pw = jnp.exp(a1 * dt * tau) * jnp.cos(ai_ref[d, g] * dt * tau - quarter)
        pws = swap(pw)
        pw_rr = jnp.where(first, pw, pws)
        pw_ii = jnp.where(first, -pws, pw)
        num = pw[1:2] - one_zero
        den = a1 * a1 + a2 * a2
        f = (num * a1 + swap(num) * a2) / den
        bb = bx_ref[d, g] * f + by_ref[d, g] * swap(f)
        cx, cy = cx_ref[d, g], cy_ref[d, g]
        cl = [cx * pw[t:t + 1] + cy * pws[t:t + 1] for t in range(q + 1)]
        inj = lambda t: bb * pw_rr[t:t + 1] + swap(bb) * pw_ii[t:t + 1]
        return pw, bb, cl, inj

    def interleave(x_f, x_b):
        return jnp.concatenate([jnp.where(first, x_f, swap(x_b)), jnp.where(first, swap(x_f), x_b)], axis=1)

    def one_group(g, carry):
        pw_f, bb_f, cl_f, inj_f = direction(0, g)
        pw_b, bb_b, cl_b, inj_b = direction(1, g)
        krow_f = nt_dot(bb_f, jnp.concatenate(cl_f[:q], axis=0))
        krow_b = nt_dot(bb_b, jnp.concatenate(cl_b[q - 1::-1], axis=0))
        skip = jnp.where(eye, jnp.broadcast_to(d_ref[g], (gc, gc)), 0.0)
        krow_f = krow_f + jnp.concatenate([skip, jnp.zeros((gc, w - gc), F32)], axis=1)
        for s in range(q):
            fwd = krow_f if s == 0 else jnp.where(lane_w >= gc * s, pltpu.roll(krow_f, gc * s, 1), 0.0)
            k = gc * (q - 1 - s)
            bwd = krow_b if k == 0 else jnp.where(lane_w < w - k, pltpu.roll(krow_b, w - k, 1), 0.0)
            t_ref[g, s * gc:(s + 1) * gc, :] = (fwd + bwd).astype(BF16)
            p_ref[g, s * gc:(s + 1) * gc, :] = interleave(inj_f(q - 1 - s), inj_b(s)).astype(BF16)
        tr_f = jnp.concatenate(cl_f[1:q + 1], axis=0).T
        tr_b = jnp.concatenate(cl_b[q:0:-1], axis=0).T
        zero = jnp.zeros((ns, w), F32)
        rf_ref[g] = jnp.concatenate([tr_f[:ns], zero, tr_f[ns:], zero], axis=0).astype(BF16)
        rb_ref[g] = jnp.concatenate([zero, tr_b[:ns], zero, tr_b[ns:]], axis=0).astype(BF16)
        lam_ref[g] = interleave(pw_f[q:q + 1], pw_b[q:q + 1])
        return carry

    lax.fori_loop(0, groups, one_group, 0)


def _ssm_matrices(a_re, a_im, log_dt, b_re, b_im, c_re, c_im, d_skip):
    ly, _, g, p = a_re.shape
    gc, w = SSM_GROUP_CH, SSM_CHUNK * SSM_GROUP_CH
    gb = min(_SSM_MAT_GROUPS, g)
    dup = lambda x, y: jnp.concatenate([x, y], axis=-1)
    row = lambda x: x[:, :, :, None, :]
    bt_re, bt_im = jnp.swapaxes(b_re, -1, -2), jnp.swapaxes(b_im, -1, -2)
    ldt = jnp.broadcast_to(log_dt[..., None], a_re.shape)
    args = [row(dup(a_re, a_re)), row(dup(a_im, -a_im)), row(dup(a_im, a_im)), row(dup(ldt, ldt)),
            dup(bt_re, bt_re), dup(-bt_im, bt_im), dup(c_re, -c_re), dup(-c_im, -c_im),
            d_skip.reshape(ly, g, 1, gc)]
    vec = pl.BlockSpec((None, 2, gb, 1, 2 * p), lambda l, i: (l, 0, i, 0, 0))
    mat = pl.BlockSpec((None, 2, gb, gc, 2 * p), lambda l, i: (l, 0, i, 0, 0))
    out = pl.BlockSpec((None, gb, w, w), lambda l, i: (l, i, 0, 0))
    shape = jax.ShapeDtypeStruct((ly, g, w, w), BF16)
    return pl.pallas_call(
        functools.partial(_ssm_mats_kernel, groups=gb),
        grid=(ly, g // gb),
        in_specs=[vec, vec, vec, vec, mat, mat, mat, mat, pl.BlockSpec((None, gb, 1, gc), lambda l, i: (l, i, 0, 0))],
        out_specs=[out, out, out, out, pl.BlockSpec((None, gb, 1, w), lambda l, i: (l, i, 0, 0))],
        out_shape=[shape, shape, shape, shape, jax.ShapeDtypeStruct((ly, g, 1, w), F32)],
        compiler_params=_cparams("parallel", "parallel"),
        name="ssm_matrices",
    )(*args)


def _prepare_params(w_mod, b_mod, w_in, w_uq, w_ukv, w_attn_o, w_conv_o, w_glu, w_ssm_o, w_out, w_ff1, w_ff2,
                    q_lora, kv_lora, conv_w, ssm_w):
    ly = w_in.shape[0]
    swap = _rope_swap_index()
    o = [0, q_lora, q_lora + kv_lora, q_lora + kv_lora + ROPE_DIM]
    o += [o[3] + conv_w, o[3] + 2 * conv_w, o[3] + 2 * conv_w + ssm_w]
    cols = lambda a, b: w_in[:, :, a:b].astype(BF16)
    kpe = cols(o[2], o[3])
    qk = NOPE_DIM + ROPE_DIM
    uq = w_uq.astype(BF16).reshape(ly, q_lora, N_HEADS, qk)
    pe = uq[..., NOPE_DIM:]
    ukv = w_ukv.astype(BF16).reshape(ly, kv_lora, N_HEADS, NOPE_DIM + V_DIM)
    return dict(
        w_mod=w_mod, b_mod=b_mod[:, None, :],
        w_a=jnp.concatenate([cols(0, o[2]), kpe, kpe[:, :, swap]], axis=2),
        w_branch=cols(o[3], w_in.shape[2]), branch_cols=[c - o[3] for c in o[3:]],
        w_uq=jnp.concatenate([uq[..., :NOPE_DIM], pe, pe[..., swap]], axis=-1).reshape(ly, q_lora, N_HEADS * HEAD_W),
        w_uk=ukv[..., :NOPE_DIM].reshape(ly, kv_lora, N_HEADS * NOPE_DIM),
        w_uv=ukv[..., NOPE_DIM:].reshape(ly, kv_lora, N_HEADS * V_DIM),
        w_attn_o=w_attn_o.astype(BF16), w_conv_o=w_conv_o.astype(BF16), w_glu=w_glu.astype(BF16),
        w_ssm_o=w_ssm_o.astype(BF16), w_out=w_out.astype(BF16), w_ff1=w_ff1.astype(BF16),
        w_ff2=w_ff2.astype(BF16),
    )


def _block(x, mod, pp, vec, ssm_m, layer, batch, seq_len, rope_tabs, cache):
    shift1, scale1, gate1, shift2, scale2, gate2 = mod
    t = x.shape[0]
    cc, ss = rope_tabs
    w_br = pp["w_branch"]
    c_conv_a, c_conv_b, c_ssm, c_gate = pp["branch_cols"]
    n_gate = w_br.shape[2] - c_gate

    q, ckv, kpe2, h = _attn_in(x, vec["g_mix_pre"], scale1, shift1, pp["w_a"], vec["g_q"], vec["g_kv"],
                               pp["w_uq"], cc, ss, layer, seq_len)
    k_new, v_new = _kv_build(ckv, kpe2, cc, ss, pp["w_uk"], pp["w_uv"], layer)
    shape3 = lambda a, b: a.reshape(b, a.shape[0] // b, a.shape[1])
    kvs = [(shape3(k_new, batch), shape3(v_new, batch))]
    if cache is not None:
        ckv_ctx, kpe_ctx, icc, iss = cache
        k_ctx, v_ctx = _kv_build(ckv_ctx, kpe_ctx, icc, iss, pp["w_uk"], pp["w_uv"], layer)
        kvs.append((shape3(k_ctx, batch), shape3(v_ctx, batch)))
    heads_per_step = min(N_HEADS, 8)
    ao = _attention(shape3(q, batch), kvs, heads_per_step).reshape(t, N_HEADS * V_DIM)

    z = _proj(h, w_br, [c_conv_a, c_conv_b], c_conv_b - c_conv_a, layer, "glu")
    cv = _conv_branch(z, vec["conv_w"], vec["conv_b"], layer, batch, seq_len)

    u = _proj(h, w_br, [c_ssm], c_gate - c_ssm, layer, "none", out_dtype=F32)
    y, fin = _ssm_scan(u, ssm_m["h0"], ssm_m["t"], ssm_m["p"], ssm_m["rf"], ssm_m["rb"], ssm_m["a"], layer, batch)
    sa = _ssm_post(y, pp["w_glu"], vec["b_glu"], layer)

    gates = _proj(h, w_br, [c_gate], n_gate, layer, "sigmoid", tn=2048)
    x = _branch_out(ao, cv, sa, gates, x, pp["w_attn_o"], pp["w_conv_o"], pp["w_ssm_o"], pp["w_out"],
                    vec["conv_ln_g"], vec["conv_ln_b"], vec["g_mix_post"], gate1, layer, seq_len)
    x = _mlp(x, vec["g_mlp_pre"], scale2, shift2, pp["w_ff1"], pp["w_ff2"], vec["g_mlp_post"], gate2, layer,
             seq_len)
    return x, ckv, kpe2[:, :ROPE_DIM], fin


def kernel(x_prompt, x_sample, cache_ckv, cache_kpe, state_ssm_re, state_ssm_im, c, c_ctx, w_mod, b_mod, g_mix_pre, g_mix_post, g_mlp_pre, g_mlp_post, w_in, g_q, w_uq, g_kv, w_ukv, w_attn_o, conv_w, conv_b, conv_ln_g, conv_ln_b, w_conv_o, ssm_a_re, ssm_a_im, ssm_log_dt, ssm_b_re, ssm_b_im, ssm_c_re, ssm_c_im, ssm_d, w_glu, b_glu, w_ssm_o, w_out, w_ff1, w_ff2):
    bp, lp, d = x_prompt.shape
    bs, ls, _ = x_sample.shape
    depth = w_in.shape[0]
    past = cache_ckv.shape[2]
    q_lora, kv_lora = g_q.shape[1], g_kv.shape[1]
    conv_ch, ssm_ch = conv_w.shape[2], ssm_d.shape[1]
    n_groups = ssm_ch // SSM_GROUP_CH

    pp = _prepare_params(w_mod, b_mod, w_in, w_uq, w_ukv, w_attn_o, w_conv_o, w_glu, w_ssm_o, w_out, w_ff1,
                         w_ff2, q_lora, kv_lora, conv_ch, ssm_ch)
    row = lambda a: a[:, None, :]
    vec = dict(g_mix_pre=row(g_mix_pre), g_mix_post=row(g_mix_post), g_mlp_pre=row(g_mlp_pre),
               g_mlp_post=row(g_mlp_post), g_q=row(g_q), g_kv=row(g_kv), conv_w=conv_w, conv_b=row(conv_b),
               conv_ln_g=row(conv_ln_g), conv_ln_b=row(conv_ln_b), b_glu=row(b_glu))
    t_m, p_m, rf_m, rb_m, a_m = _ssm_matrices(ssm_a_re, ssm_a_im, ssm_log_dt, ssm_b_re, ssm_b_im, ssm_c_re, ssm_c_im,
                                       ssm_d)

    n_cond = 1 + bs
    cond = jnp.concatenate([c_ctx[None, :], c, jnp.zeros((-n_cond % 8, d), F32)], axis=0)
    mod_all = _modulation(cond, pp["w_mod"], pp["b_mod"])

    rope_s = _rope_tables(ls)
    ident_p = _identity_tables(min(lp, 256))
    ident_c = _identity_tables(min(past, 256))
    zero_h0 = jnp.zeros((n_groups, bp, 4 * SSM_STATE), F32)

    xp = x_prompt.reshape(bp * lp, d)
    xs = x_sample.reshape(bs * ls, d)
    ckvs, kpes, fins = [], [], []
    for l in range(depth):
        mods = jnp.split(mod_all[l], 6, axis=-1)
        mod_ctx = [m[0:1, None, :] for m in mods]
        mod_lat = [m[1:n_cond, None, :] for m in mods]
        ssm_p = dict(t=t_m, p=p_m, rf=rf_m, rb=rb_m, a=a_m, h0=zero_h0)
        xp, ckv, kpe, fin = _block(xp, mod_ctx, pp, vec, ssm_p, l, bp, lp, ident_p, None)
        ckvs.append(ckv.reshape(bp, lp, kv_lora))
        kpes.append(kpe.reshape(bp, lp, ROPE_DIM))
        fins.append(fin)
        sr, si = state_ssm_re[:, l], state_ssm_im[:, l]
        h0 = jnp.concatenate([sr[:, 0], sr[:, 1], si[:, 0], si[:, 1]], axis=-1).transpose(1, 0, 2)
        kpe_ctx = jnp.pad(cache_kpe[:, l].reshape(bs * past, ROPE_DIM), ((0, 0), (0, ROPE_DIM)))
        cache = (cache_ckv[:, l].reshape(bs * past, kv_lora), kpe_ctx) + ident_c
        ssm_s = dict(t=t_m, p=p_m, rf=rf_m, rb=rb_m, a=a_m, h0=h0)
        xs = _block(xs, mod_lat, pp, vec, ssm_s, l, bs, ls, rope_s, cache)[0]

    new_ckv = jnp.stack(ckvs, axis=1)
    new_kpe = jnp.stack(kpes, axis=1)
    fin = jnp.stack(fins, axis=0)
    fin = fin.reshape(depth, n_groups, bp, 2, 2, SSM_STATE).transpose(3, 2, 0, 4, 1, 5)
    return (xp.reshape(bp, lp, d), xs.reshape(bs, ls, d), new_ckv, new_kpe, fin[0], fin[1])
```

```python
import functools
import math

import jax
import jax.numpy as jnp
from jax import lax
from jax.experimental import pallas as pl
from jax.experimental.pallas import tpu as pltpu

N_HEADS = 16
NOPE_DIM = 128
ROPE_DIM = 64
V_DIM = 128
GRID_W = 64
ROPE_BASE = 10000.0
SSM_GROUP_CH = 16
SSM_STATE = 64
EPS = 1e-6

HEAD_W = NOPE_DIM + 2 * ROPE_DIM
_Q_SCALE = math.log2(math.e) / math.sqrt(NOPE_DIM + ROPE_DIM)
SSM_CHUNK = 16
V7X_VMEM_LIMIT_BYTES = 56 * 1024 * 1024
_LANES = 128

F32 = jnp.float32
BF16 = jnp.bfloat16


def _cparams(*sem):
    return pltpu.CompilerParams(dimension_semantics=sem, vmem_limit_bytes=V7X_VMEM_LIMIT_BYTES)


def _tile(n, target):
    if n <= target:
        return n
    for t in range(target, 7, -1):
        if n % t == 0 and t % 8 == 0:
            return t
    return n


def _rms(x, g):
    return x * lax.rsqrt(jnp.mean(x * x, axis=-1, keepdims=True) + EPS) * g


def _rmsmod(x, g, scale, shift):
    return _rms(x, g) * (1.0 + scale) + shift


def _mod_kernel(c_ref, w_ref, b_ref, o_ref):
    h = jax.nn.silu(c_ref[...]).astype(BF16)
    o_ref[...] = jnp.dot(h, w_ref[...].astype(BF16), preferred_element_type=F32) + b_ref[...]


def _modulation(cond, w_mod, b_mod):
    ly, d, n = w_mod.shape
    r = cond.shape[0]
    tn = _tile(n, 1536)
    return pl.pallas_call(
        _mod_kernel,
        grid=(ly, n // tn),
        in_specs=[
            pl.BlockSpec((r, d), lambda l, j: (0, 0)),
            pl.BlockSpec((None, d, tn), lambda l, j: (l, 0, j)),
            pl.BlockSpec((None, 1, tn), lambda l, j: (l, 0, j)),
        ],
        out_specs=pl.BlockSpec((None, r, tn), lambda l, j: (l, 0, j)),
        out_shape=jax.ShapeDtypeStruct((ly, r, n), F32),
        compiler_params=_cparams("parallel", "parallel"),
        name="modulation",
    )(cond, w_mod, b_mod)


def _proj_kernel(h_ref, *rest, n_w, epilogue):
    w_refs, o_ref = rest[:n_w], rest[n_w]
    h = h_ref[...]
    a = jnp.dot(h, w_refs[0][...], preferred_element_type=F32)
    if epilogue == "glu":
        a = a * jax.nn.sigmoid(jnp.dot(h, w_refs[1][...], preferred_element_type=F32))
    elif epilogue == "sigmoid":
        a = jax.nn.sigmoid(a)
    o_ref[...] = a.astype(o_ref.dtype)


def _proj(h, w, col_starts, n, layer, epilogue, out_dtype=BF16, tm=1024, tn=1024):
    t, d = h.shape
    tm = _tile(t, tm)
    tn = next(c for c in range(min(n, tn // len(col_starts)), 0, -_LANES)
              if n % c == 0 and all(s % c == 0 for s in col_starts))
    w_specs = [pl.BlockSpec((None, d, tn), functools.partial(lambda i, j, off: (layer, 0, off + j), off=c // tn))
               for c in col_starts]
    return pl.pallas_call(
        functools.partial(_proj_kernel, n_w=len(col_starts), epilogue=epilogue),
        grid=(t // tm, n // tn),
        in_specs=[pl.BlockSpec((tm, d), lambda i, j: (i, 0))] + w_specs,
        out_specs=pl.BlockSpec((tm, tn), lambda i, j: (i, j)),
        out_shape=jax.ShapeDtypeStruct((t, n), out_dtype),
        compiler_params=_cparams("parallel", "parallel"),
        name="proj_" + epilogue,
    )(h, *([w] * len(col_starts)))


def _attn_in_kernel(x_ref, g_ref, sc_ref, sh_ref, wa_ref, gq_ref, gkv_ref, wuq_ref, cc_ref, ss_ref,
                    q_ref, ckv_ref, kpe_ref, h_ref, *, q_lora, kv_lora):
    h = _rmsmod(x_ref[...], g_ref[...], sc_ref[0], sh_ref[0]).astype(BF16)
    h_ref[...] = h
    a = jnp.dot(h, wa_ref[...], preferred_element_type=F32)
    cq = _rms(a[:, :q_lora], gq_ref[...]).astype(BF16)
    ckv_ref[...] = _rms(a[:, q_lora:q_lora + kv_lora], gkv_ref[...])
    kpe_ref[...] = a[:, q_lora + kv_lora:]
    q = jnp.dot(cq, wuq_ref[...], preferred_element_type=F32) * _Q_SCALE
    cc, ss = cc_ref[...], ss_ref[...]
    for hd in range(N_HEADS):
        lo = hd * HEAD_W
        q_ref[:, lo:lo + NOPE_DIM] = q[:, lo:lo + NOPE_DIM].astype(BF16)
        v = q[:, lo + NOPE_DIM:lo + HEAD_W]
        q_ref[:, lo + NOPE_DIM:lo + HEAD_W] = (v * cc + pltpu.roll(v, ROPE_DIM, 1) * ss).astype(BF16)


def _attn_in(x, g, scale, shift, w_a, g_q, g_kv, w_uq, cc, ss, layer, seq_len, tm=256):
    t, d = x.shape
    na = w_a.shape[2]
    q_lora, kv_lora = g_q.shape[2], g_kv.shape[2]
    nq = w_uq.shape[2]
    tm = _tile(seq_len, tm)
    per_seq = seq_len // tm
    nb = scale.shape[0]
    bidx = (lambda i: i // per_seq) if nb > 1 else (lambda i: 0)
    ntab = cc.shape[0] // tm
    lsel = lambda i: (layer, 0, 0)
    return pl.pallas_call(
        functools.partial(_attn_in_kernel, q_lora=q_lora, kv_lora=kv_lora),
        grid=(t // tm,),
        in_specs=[
            pl.BlockSpec((tm, d), lambda i: (i, 0)),
            pl.BlockSpec((None, 1, d), lsel),
            pl.BlockSpec((1, 1, d), lambda i: (bidx(i), 0, 0)),
            pl.BlockSpec((1, 1, d), lambda i: (bidx(i), 0, 0)),
            pl.BlockSpec((None, d, na), lsel),
            pl.BlockSpec((None, 1, q_lora), lsel),
            pl.BlockSpec((None, 1, kv_lora), lsel),
            pl.BlockSpec((None, q_lora, nq), lsel),
            pl.BlockSpec((tm, 2 * ROPE_DIM), lambda i: (i % ntab, 0)),
            pl.BlockSpec((tm, 2 * ROPE_DIM), lambda i: (i % ntab, 0)),
        ],
        out_specs=[
            pl.BlockSpec((tm, nq), lambda i: (i, 0)),
            pl.BlockSpec((tm, kv_lora), lambda i: (i, 0)),
            pl.BlockSpec((tm, 2 * ROPE_DIM), lambda i: (i, 0)),
            pl.BlockSpec((tm, d), lambda i: (i, 0)),
        ],
        out_shape=[
            jax.ShapeDtypeStruct((t, nq), BF16),
            jax.ShapeDtypeStruct((t, kv_lora), F32),
            jax.ShapeDtypeStruct((t, 2 * ROPE_DIM), F32),
            jax.ShapeDtypeStruct((t, d), BF16),
        ],
        compiler_params=_cparams("parallel"),
        name="attn_in",
    )(x, g, scale, shift, w_a, g_q, g_kv, w_uq, cc, ss)


_NT_DIMS = (((1,), (1,)), ((), ()))


def _kv_build_kernel(ckv_ref, kpe_ref, cc_ref, ss_ref, wuk_ref, wuv_ref, k_ref, v_ref):
    c = ckv_ref[...].astype(BF16)
    kn = jnp.dot(c, wuk_ref[...], preferred_element_type=F32)
    v_ref[...] = jnp.dot(c, wuv_ref[...], preferred_element_type=F32).astype(BF16)
    kp = kpe_ref[...]
    kp = (kp * cc_ref[...] + pltpu.roll(kp, ROPE_DIM, 1) * ss_ref[...]).astype(BF16)
    for hd in range(N_HEADS):
        k_ref[:, hd * HEAD_W:hd * HEAD_W + NOPE_DIM] = kn[:, hd * NOPE_DIM:(hd + 1) * NOPE_DIM].astype(BF16)
        k_ref[:, hd * HEAD_W + NOPE_DIM:(hd + 1) * HEAD_W] = kp


def _kv_build(ckv, kpe2, cc, ss, w_uk, w_uv, layer, tm=256):
    t, kvl = ckv.shape
    tm = _tile(cc.shape[0], tm)
    ntab = cc.shape[0] // tm
    nk, nv = N_HEADS * HEAD_W, w_uv.shape[2]
    lsel = lambda i: (layer, 0, 0)
    return pl.pallas_call(
        _kv_build_kernel,
        grid=(t // tm,),
        in_specs=[
            pl.BlockSpec((tm, kvl), lambda i: (i, 0)),
            pl.BlockSpec((tm, 2 * ROPE_DIM), lambda i: (i, 0)),
            pl.BlockSpec((tm, 2 * ROPE_DIM), lambda i: (i % ntab, 0)),
            pl.BlockSpec((tm, 2 * ROPE_DIM), lambda i: (i % ntab, 0)),
            pl.BlockSpec((None, kvl, w_uk.shape[2]), lsel),
            pl.BlockSpec((None, kvl, nv), lsel),
        ],
        out_specs=[pl.BlockSpec((tm, nk), lambda i: (i, 0)), pl.BlockSpec((tm, nv), lambda i: (i, 0))],
        out_shape=[jax.ShapeDtypeStruct((t, nk), BF16), jax.ShapeDtypeStruct((t, nv), BF16)],
        compiler_params=_cparams("parallel"),
        name="kv_build",
    )(ckv, kpe2, cc, ss, w_uk, w_uv)


def _attn_kernel(q_ref, *rest, n_src, heads):
    kv_refs, o_ref = rest[:2 * n_src], rest[2 * n_src]
    for hd in range(heads):
        q = q_ref[:, hd * HEAD_W:(hd + 1) * HEAD_W]
        s = [lax.dot_general(q, kv_refs[2 * i][:, hd * HEAD_W:(hd + 1) * HEAD_W], _NT_DIMS,
                             preferred_element_type=F32) for i in range(n_src)]
        m = jnp.max(s[0], axis=-1, keepdims=True)
        for si in s[1:]:
            m = jnp.maximum(m, jnp.max(si, axis=-1, keepdims=True))
        acc, den = None, None
        for i in range(n_src):
            p = jnp.exp2(s[i] - m)
            li = jnp.sum(p, axis=-1, keepdims=True)
            oi = jnp.dot(p.astype(BF16), kv_refs[2 * i + 1][:, hd * V_DIM:(hd + 1) * V_DIM],
                         preferred_element_type=F32)
            acc = oi if acc is None else acc + oi
            den = li if den is None else den + li
        o_ref[:, hd * V_DIM:(hd + 1) * V_DIM] = (acc / den).astype(BF16)


def _attention(q, kvs, heads_per_step, tq=256):
    b, sq, _ = q.shape
    tq = _tile(sq, tq)
    hb = heads_per_step
    in_specs = [pl.BlockSpec((None, tq, hb * HEAD_W), lambda bi, hi, qi: (bi, qi, hi))]
    args = [q]
    for k, v in kvs:
        sk = k.shape[1]
        in_specs.append(pl.BlockSpec((None, sk, hb * HEAD_W), lambda bi, hi, qi: (bi, 0, hi)))
        in_specs.append(pl.BlockSpec((None, sk, hb * V_DIM), lambda bi, hi, qi: (bi, 0, hi)))
        args += [k, v]
    return pl.pallas_call(
        functools.partial(_attn_kernel, n_src=len(kvs), heads=hb),
        grid=(b, N_HEADS // hb, sq // tq),
        in_specs=in_specs,
        out_specs=pl.BlockSpec((None, tq, hb * V_DIM), lambda bi, hi, qi: (bi, qi, hi)),
        out_shape=jax.ShapeDtypeStruct((b, sq, N_HEADS * V_DIM), BF16),
        compiler_params=_cparams("parallel", "parallel", "parallel"),
        name="attention",
    )(*args)


_CONV_ROWS = 64
_CONV_LANES = 256
_CONV_HALO = 16
_SUBLANES = 8


def _conv_kernel(z_ref, w_ref, b_ref, o_ref, zs_ref, wb_ref, *, seq_len, taps):
    lanes = z_ref.shape[-1]
    pad = (taps - 1) // 2
    halo = jnp.zeros((_CONV_HALO, lanes), F32)
    zp = jnp.concatenate([halo, z_ref[...].astype(F32), halo], axis=0)
    span = seq_len + 2 * _CONV_HALO - _SUBLANES
    for j in range(_SUBLANES):
        zs_ref[j, 0:span, :] = zp[j:j + span, :]
    for k in range(taps):
        wb_ref[k] = jnp.broadcast_to(w_ref[k:k + 1, :], (_SUBLANES, lanes))
    bias = jnp.broadcast_to(b_ref[...], (_SUBLANES, lanes))
    n_sub = _CONV_ROWS // _SUBLANES

    def conv_step(it, carry):
        r0 = pl.multiple_of(it * _CONV_ROWS, _CONV_ROWS)
        acc = [bias] * n_sub
        for k in range(taps):
            off = _CONV_HALO - pad + k
            j, base = off % _SUBLANES, off - off % _SUBLANES
            win = zs_ref[j, pl.ds(r0 + base, _CONV_ROWS), :]
            wk = wb_ref[k]
            acc = [acc[p] + wk * win[p * _SUBLANES:(p + 1) * _SUBLANES, :] for p in range(n_sub)]
        o_ref[pl.ds(r0, _CONV_ROWS), :] = jnp.concatenate(acc, axis=0).astype(BF16)
        return carry

    lax.fori_loop(0, seq_len // _CONV_ROWS, conv_step, 0)


def _conv_branch(z, conv_w, conv_b, layer, batch, seq_len):
    t, ch = z.shape
    taps = conv_w.shape[1]
    lanes = _tile(ch, _CONV_LANES)
    return pl.pallas_call(
        functools.partial(_conv_kernel, seq_len=seq_len, taps=taps),
        grid=(batch, ch // lanes),
        in_specs=[
            pl.BlockSpec((seq_len, lanes), lambda b, c: (b, c)),
            pl.BlockSpec((None, taps, lanes), lambda b, c: (layer, 0, c)),
            pl.BlockSpec((None, 1, lanes), lambda b, c: (layer, 0, c)),
        ],
        out_specs=pl.BlockSpec((seq_len, lanes), lambda b, c: (b, c)),
        out_shape=jax.ShapeDtypeStruct((t, ch), BF16),
        scratch_shapes=[pltpu.VMEM((_SUBLANES, seq_len + 2 * _CONV_HALO, lanes), F32),
                        pltpu.VMEM((taps, _SUBLANES, lanes), F32)],
        compiler_params=_cparams("parallel", "parallel"),
        name="depthwise_conv",
    )(z, conv_w, conv_b)


_SSM_TILE_GROUPS = _LANES // SSM_GROUP_CH
_SSM_MOVE_ROWS = 64
_SSM_SCAN_GROUPS = 2


def _ssm_kernel(x_ref, h0_ref, t_ref, p_ref, rf_ref, rb_ref, a_ref, y_ref, fin_ref,
                u_scr, y_scr, ds_scr, sf_scr, sb_scr, *, n_chunks, batch):
    m = n_chunks * batch
    ng, q, gc = _SSM_TILE_GROUPS, SSM_CHUNK, SSM_GROUP_CH
    half = 2 * SSM_STATE
    rr = min(_SSM_MOVE_ROWS, m)
    blk = lax.broadcasted_iota(jnp.int32, (rr, _LANES), 1) // gc
    is_fwd = lax.broadcasted_iota(jnp.int32, (batch, half), 1) < SSM_STATE

    def block_transpose(vs):
        vs = list(vs)
        d = ng // 2
        while d >= 1:
            odd = (blk // d) % 2 == 1
            nxt = list(vs)
            for a in range(ng):
                if (a // d) % 2 == 0:
                    lo, hi = vs[a], vs[a + d]
                    nxt[a] = jnp.where(odd, pltpu.roll(hi, d * gc, 1), lo)
                    nxt[a + d] = jnp.where(odd, hi, pltpu.roll(lo, _LANES - d * gc, 1))
            vs = nxt
            d //= 2
        return vs

    def gather_in(it, carry):
        r0 = pl.multiple_of(it * rr, rr)
        halves = [block_transpose([x_ref[pl.ds(r0, rr), h * ng + j, :] for j in range(ng)])
                  for h in range(q // ng)]
        for g in range(ng):
            u_scr[g, pl.ds(r0, rr), :] = jnp.concatenate([hv[g] for hv in halves], axis=1).astype(BF16)
        return carry

    lax.fori_loop(0, m // rr, gather_in, 0)

    def group_pair(it, carry):
        gs = [_SSM_SCAN_GROUPS * it + k for k in range(_SSM_SCAN_GROUPS)]
        a_re, a_im, s_re, s_im = [], [], [], []
        for k, g in enumerate(gs):
            ds_scr[k] = jnp.dot(u_scr[g], p_ref[g], preferred_element_type=F32).reshape(batch, n_chunks, 2 * half)
            a, h0 = a_ref[g], h0_ref[g]
            a_re.append(a[:, :half]); a_im.append(a[:, half:])
            s_re.append(h0[:, :half]); s_im.append(h0[:, half:])
        for c in range(n_chunks):
            cb = n_chunks - 1 - c
            for k in range(_SSM_SCAN_GROUPS):
                state = jnp.concatenate([s_re[k], s_im[k]], axis=1)
                sf_scr[k, c] = state
                sb_scr[k, cb] = state
                d_re = jnp.where(is_fwd, ds_scr[k, :, c, :half], ds_scr[k, :, cb, :half])
                d_im = jnp.where(is_fwd, ds_scr[k, :, c, half:], ds_scr[k, :, cb, half:])
                s_re[k], s_im[k] = (a_re[k] * s_re[k] - a_im[k] * s_im[k] + d_re,
                                    a_re[k] * s_im[k] + a_im[k] * s_re[k] + d_im)
        for k, g in enumerate(gs):
            fin_ref[g] = jnp.concatenate([s_re[k], s_im[k]], axis=1)
            by_batch = lambda scr: jnp.concatenate([scr[k, :, b, :] for b in range(batch)], axis=0).astype(BF16)
            y = jnp.dot(u_scr[g], t_ref[g], preferred_element_type=F32)
            y = y + jnp.dot(by_batch(sf_scr), rf_ref[g], preferred_element_type=F32)
            y = y + jnp.dot(by_batch(sb_scr), rb_ref[g], preferred_element_type=F32)
            y_scr[g] = y
        return carry

    lax.fori_loop(0, ng // _SSM_SCAN_GROUPS, group_pair, 0)

    def scatter_out(it, carry):
        r0 = pl.multiple_of(it * rr, rr)
        ys = [y_scr[g, pl.ds(r0, rr), :] for g in range(ng)]
        for h in range(q // ng):
            toks = block_transpose([y[:, h * _LANES:(h + 1) * _LANES] for y in ys])
            for j in range(ng):
                y_ref[pl.ds(r0, rr), h * ng + j, :] = toks[j]
        return carry

    lax.fori_loop(0, m // rr, scatter_out, 0)


def _ssm_scan(u, h0, t_m, p_m, rf_m, rb_m, a_m, layer, batch):
    t, ch = u.shape
    m = t // SSM_CHUNK
    n_chunks = m // batch
    ng, w = _SSM_TILE_GROUPS, SSM_CHUNK * SSM_GROUP_CH
    g_all = ch // SSM_GROUP_CH
    tok = pl.BlockSpec((m, SSM_CHUNK, _LANES), lambda i: (0, 0, i))
    wsel = lambda i: (layer, i, 0, 0)
    mat = pl.BlockSpec((None, ng, w, w), wsel)
    st = pl.BlockSpec((ng, batch, w), lambda i: (i, 0, 0))
    y, fin = pl.pallas_call(
        functools.partial(_ssm_kernel, n_chunks=n_chunks, batch=batch),
        grid=(g_all // ng,),
        in_specs=[tok, st, mat, mat, mat, mat, pl.BlockSpec((None, ng, 1, w), wsel)],
        out_specs=[tok, st],
        out_shape=[jax.ShapeDtypeStruct((m, SSM_CHUNK, ch), F32), jax.ShapeDtypeStruct((g_all, batch, w), F32)],
        scratch_shapes=[pltpu.VMEM((ng, m, w), BF16), pltpu.VMEM((ng, m, w), F32),
                        pltpu.VMEM((_SSM_SCAN_GROUPS, batch, n_chunks, w), F32),
                        pltpu.VMEM((_SSM_SCAN_GROUPS, n_chunks, batch, w), F32),
                        pltpu.VMEM((_SSM_SCAN_GROUPS, n_chunks, batch, w), F32)],
        compiler_params=_cparams("parallel"),
        name="ssm_chunk_scan",
    )(u.reshape(m, SSM_CHUNK, ch), h0, t_m, p_m, rf_m, rb_m, a_m)
    return y.reshape(t, ch), fin


def _ssm_post_kernel(y_ref, w_ref, b_ref, o_ref):
    g = jax.nn.gelu(y_ref[...].astype(F32))
    s = jnp.dot(g.astype(BF16), w_ref[...], preferred_element_type=F32) + b_ref[...]
    o_ref[...] = (g * jax.nn.sigmoid(s)).astype(BF16)


def _ssm_post(y, w_glu, b_glu, layer, tm=512):
    t, w = y.shape
    tm = _tile(t, tm)
    lsel = lambda i: (layer, 0, 0)
    return pl.pallas_call(
        _ssm_post_kernel,
        grid=(t // tm,),
        in_specs=[pl.BlockSpec((tm, w), lambda i: (i, 0)), pl.BlockSpec((None, w, w), lsel),
                  pl.BlockSpec((None, 1, w), lsel)],
        out_specs=pl.BlockSpec((tm, w), lambda i: (i, 0)),
        out_shape=jax.ShapeDtypeStruct((t, w), BF16),
        compiler_params=_cparams("parallel"),
        name="ssm_gelu_glu",
    )(y, w_glu, b_glu)


def _accumulate_then(acc_ref, part, k, nk, finish):
    if nk == 1:
        finish(part)
        return

    @pl.when(k == 0)
    def _():
        acc_ref[...] = part

    @pl.when(jnp.logical_and(k > 0, k < nk - 1))
    def _():
        acc_ref[...] += part

    @pl.when(k == nk - 1)
    def _():
        finish(acc_ref[...] + part)


def _branch_out_kernel(ao_ref, cv_ref, sa_ref, gates_ref, wa_ref, wc_ref, ws_ref, wo_ref,
                       lg_ref, lb_ref, g_ref, gate_ref, x_ref, o_ref):
    d = x_ref.shape[-1]
    v = cv_ref[...].astype(F32)
    vc = v - jnp.mean(v, axis=-1, keepdims=True)
    var = jnp.mean(vc * vc, axis=-1, keepdims=True)
    ca = jax.nn.silu(vc * lax.rsqrt(var + EPS) * lg_ref[...] + lb_ref[...]).astype(BF16)
    mix = gates_ref[:, 0:d].astype(F32) * jnp.dot(ao_ref[...], wa_ref[...], preferred_element_type=F32)
    mix = mix + gates_ref[:, d:2 * d].astype(F32) * jnp.dot(ca, wc_ref[...], preferred_element_type=F32)
    mix = mix + gates_ref[:, 2 * d:].astype(F32) * jnp.dot(sa_ref[...], ws_ref[...], preferred_element_type=F32)
    total = jnp.dot(mix.astype(BF16), wo_ref[...], preferred_element_type=F32)
    o_ref[...] = x_ref[...] + gate_ref[0] * _rms(total, g_ref[...])


def _branch_out(ao, cv, sa, gates, x, w_attn_o, w_conv_o, w_ssm_o, w_out, ln_g, ln_b, g_post, gate, layer,
                seq_len, tm=256):
    t, d = x.shape
    nb = gate.shape[0]
    tm = _tile(t if nb == 1 else seq_len, tm)
    per_seq = seq_len // tm
    bidx = (lambda i: i // per_seq) if nb > 1 else (lambda i: 0)
    act = lambda a: pl.BlockSpec((tm, a.shape[1]), lambda i: (i, 0))
    fixed = lambda a: pl.BlockSpec((None,) + a.shape[1:], lambda i: (layer, 0, 0), pipeline_mode=pl.Buffered(1))
    return pl.pallas_call(
        _branch_out_kernel,
        grid=(t // tm,),
        in_specs=[act(ao), act(cv), act(sa), act(gates),
                  fixed(w_attn_o), fixed(w_conv_o), fixed(w_ssm_o), fixed(w_out),
                  fixed(ln_g), fixed(ln_b), fixed(g_post),
                  pl.BlockSpec((1, 1, d), lambda i: (bidx(i), 0, 0)), act(x)],
        out_specs=act(x),
        out_shape=jax.ShapeDtypeStruct((t, d), F32),
        compiler_params=_cparams("parallel"),
        name="branch_out",
    )(ao, cv, sa, gates, w_attn_o, w_conv_o, w_ssm_o, w_out, ln_g, ln_b, g_post, gate, x)


def _mlp_kernel(x_ref, gpre_ref, sc_ref, sh_ref, w1_ref, w2_ref, gpost_ref, gate_ref, o_ref, h_ref, acc_ref, *,
                nk):
    k = pl.program_id(1)

    @pl.when(k == 0)
    def _():
        h_ref[...] = _rmsmod(x_ref[...], gpre_ref[...], sc_ref[0], sh_ref[0]).astype(BF16)

    f = jnp.square(jnp.maximum(jnp.dot(h_ref[...], w1_ref[...], preferred_element_type=F32), 0.0))
    part = jnp.dot(f.astype(BF16), w2_ref[...], preferred_element_type=F32)

    def finish(total):
        o_ref[...] = x_ref[...] + gate_ref[0] * _rms(total, gpost_ref[...])

    _accumulate_then(acc_ref, part, k, nk, finish)


def _mlp(x, g_pre, scale, shift, w1, w2, g_post, gate, layer, seq_len, tm=512, tf=1024):
    t, d = x.shape
    ff = w1.shape[2]
    nb = gate.shape[0]
    tm, tf = _tile(t if nb == 1 else seq_len, tm), _tile(ff, tf)
    nk = ff // tf
    per_seq = seq_len // tm
    bidx = (lambda i: i // per_seq) if nb > 1 else (lambda i: 0)
    vec = pl.BlockSpec((None, 1, d), lambda i, k: (layer, 0, 0))
    mod = pl.BlockSpec((1, 1, d), lambda i, k: (bidx(i), 0, 0))
    return pl.pallas_call(
        functools.partial(_mlp_kernel, nk=nk),
        grid=(t // tm, nk),
        in_specs=[pl.BlockSpec((tm, d), lambda i, k: (i, 0)), vec, mod, mod,
                  pl.BlockSpec((None, d, tf), lambda i, k: (layer, 0, k)),
                  pl.BlockSpec((None, tf, d), lambda i, k: (layer, k, 0)),
                  vec, mod],
        out_specs=pl.BlockSpec((tm, d), lambda i, k: (i, 0)),
        out_shape=jax.ShapeDtypeStruct((t, d), F32),
        scratch_shapes=[pltpu.VMEM((tm, d), BF16), pltpu.VMEM((tm, d), F32)],
        compiler_params=_cparams("parallel", "arbitrary"),
        name="mlp_relu2",
    )(x, g_pre, scale, shift, w1, w2, g_post, gate)


def _rope_swap_index():
    q = ROPE_DIM // 4
    return jnp.concatenate([jnp.arange(q, 2 * q), jnp.arange(0, q), jnp.arange(3 * q, 4 * q), jnp.arange(2 * q, 3 * q)])


def _rope_tables(n_tok):
    pairs = ROPE_DIM // 4
    pos = jnp.arange(n_tok)
    row = (pos // GRID_W).astype(F32)
    col = (pos % GRID_W).astype(F32)
    inv = ROPE_BASE ** (-jnp.arange(pairs, dtype=F32) / pairs)
    ang = jnp.stack([row[:, None] * inv, col[:, None] * inv], axis=1)
    cos, sin = jnp.cos(ang), jnp.sin(ang)
    cc = jnp.stack([cos, cos], axis=2).reshape(n_tok, ROPE_DIM)
    ss = jnp.stack([-sin, sin], axis=2).reshape(n_tok, ROPE_DIM)
    z = jnp.zeros((n_tok, ROPE_DIM), F32)
    return jnp.concatenate([cc, z], axis=1), jnp.concatenate([ss, z], axis=1)


def _identity_tables(n_tok):
    one = jnp.ones((n_tok, ROPE_DIM), F32)
    z = jnp.zeros((n_tok, ROPE_DIM), F32)
    return jnp.concatenate([one, z], axis=1), jnp.zeros((n_tok, 2 * ROPE_DIM), F32)


_SSM_MAT_GROUPS = 8


def _ssm_mats_kernel(a1_ref, a2_ref, ai_ref, ldt_ref, bx_ref, by_ref, cx_ref, cy_ref, d_ref,
                     t_ref, p_ref, rf_ref, rb_ref, lam_ref, *, groups):
    q, gc, ns = SSM_CHUNK, SSM_GROUP_CH, SSM_STATE
    w = q * gc
    hp = lax.Precision.HIGHEST
    n_pow = 24
    tau = lax.broadcasted_iota(jnp.int32, (n_pow, 2 * ns), 0).astype(F32)
    first = lax.broadcasted_iota(jnp.int32, (1, 2 * ns), 1) < ns
    quarter = jnp.where(first, 0.0, 0.5 * math.pi)
    one_zero = jnp.where(first, 1.0, 0.0)
    lane_w = lax.broadcasted_iota(jnp.int32, (gc, w), 1)
    eye = (lax.broadcasted_iota(jnp.int32, (gc, gc), 0) == lax.broadcasted_iota(jnp.int32, (gc, gc), 1))
    swap = lambda x: pltpu.roll(x, ns, 1)
    nt_dot = lambda x, y: lax.dot_general(x, y, (((1,), (1,)), ((), ())), precision=hp,
                                          preferred_element_type=F32)

    def direction(d, g):
        dt = jnp.exp(ldt_ref[d, g])
        a1, a2 = a1_ref[d, g], a2_ref[d, g]
        pw = jnp.exp(a1 * dt * tau) * jnp.cos(ai_ref[d, g] * dt * tau - quarter)
        pws = swap(pw)
        pw_rr = jnp.where(first, pw, pws)
        pw_ii = jnp.where(first, -pws, pw)
        num = pw[1:2] - one_zero
        den = a1 * a1 + a2 * a2
        f = (num * a1 + swap(num) * a2) / den
        bb = bx_ref[d, g] * f + by_ref[d, g] * swap(f)
        cx, cy = cx_ref[d, g], cy_ref[d, g]
        cl = [cx * pw[t:t + 1] + cy * pws[t:t + 1] for t in range(q + 1)]
        inj = lambda t: bb * pw_rr[t:t + 1] + swap(bb) * pw_ii[t:t + 1]
        return pw, bb, cl, inj

    def interleave(x_f, x_b):
        return jnp.concatenate([jnp.where(first, x_f, swap(x_b)), jnp.where(first, swap(x_f), x_b)], axis=1)

    def one_group(g, carry):
        pw_f, bb_f, cl_f, inj_f = direction(0, g)
        pw_b, bb_b, cl_b, inj_b = direction(1, g)
        krow_f = nt_dot(bb_f, jnp.concatenate(cl_f[:q], axis=0))
        krow_b = nt_dot(bb_b, jnp.concatenate(cl_b[q - 1::-1], axis=0))
        skip = jnp.where(eye, jnp.broadcast_to(d_ref[g], (gc, gc)), 0.0)
        krow_f = krow_f + jnp.concatenate([skip, jnp.zeros((gc, w - gc), F32)], axis=1)
        for s in range(q):
            fwd = krow_f if s == 0 else jnp.where(lane_w >= gc * s, pltpu.roll(krow_f, gc * s, 1), 0.0)
            k = gc * (q - 1 - s)
            bwd = krow_b if k == 0 else jnp.where(lane_w < w - k, pltpu.roll(krow_b, w - k, 1), 0.0)
            t_ref[g, s * gc:(s + 1) * gc, :] = (fwd + bwd).astype(BF16)
            p_ref[g, s * gc:(s + 1) * gc, :] = interleave(inj_f(q - 1 - s), inj_b(s)).astype(BF16)
        tr_f = jnp.concatenate(cl_f[1:q + 1], axis=0).T
        tr_b = jnp.concatenate(cl_b[q:0:-1], axis=0).T
        zero = jnp.zeros((ns, w), F32)
        rf_ref[g] = jnp.concatenate([tr_f[:ns], zero, tr_f[ns:], zero], axis=0).astype(BF16)
        rb_ref[g] = jnp.concatenate([zero, tr_b[:ns], zero, tr_b[ns:]], axis=0).astype(BF16)
        lam_ref[g] = interleave(pw_f[q:q + 1], pw_b[q:q + 1])
        return carry

    lax.fori_loop(0, groups, one_group, 0)


def _ssm_matrices(a_re, a_im, log_dt, b_re, b_im, c_re, c_im, d_skip):
    ly, _, g, p = a_re.shape
    gc, w = SSM_GROUP_CH, SSM_CHUNK * SSM_GROUP_CH
    gb = min(_SSM_MAT_GROUPS, g)
    dup = lambda x, y: jnp.concatenate([x, y], axis=-1)
    row = lambda x: x[:, :, :, None, :]
    bt_re, bt_im = jnp.swapaxes(b_re, -1, -2), jnp.swapaxes(b_im, -1, -2)
    ldt = jnp.broadcast_to(log_dt[..., None], a_re.shape)
    args = [row(dup(a_re, a_re)), row(dup(a_im, -a_im)), row(dup(a_im, a_im)), row(dup(ldt, ldt)),
            dup(bt_re, bt_re), dup(-bt_im, bt_im), dup(c_re, -c_re), dup(-c_im, -c_im),
            d_skip.reshape(ly, g, 1, gc)]
    vec = pl.BlockSpec((None, 2, gb, 1, 2 * p), lambda l, i: (l, 0, i, 0, 0))
    mat = pl.BlockSpec((None, 2, gb, gc, 2 * p), lambda l, i: (l, 0, i, 0, 0))
    out = pl.BlockSpec((None, gb, w, w), lambda l, i: (l, i, 0, 0))
    shape = jax.ShapeDtypeStruct((ly, g, w, w), BF16)
    return pl.pallas_call(
        functools.partial(_ssm_mats_kernel, groups=gb),
        grid=(ly, g // gb),
        in_specs=[vec, vec, vec, vec, mat, mat, mat, mat, pl.BlockSpec((None, gb, 1, gc), lambda l, i: (l, i, 0, 0))],
        out_specs=[out, out, out, out, pl.BlockSpec((None, gb, 1, w), lambda l, i: (l, i, 0, 0))],
        out_shape=[shape, shape, shape, shape, jax.ShapeDtypeStruct((ly, g, 1, w), F32)],
        compiler_params=_cparams("parallel", "parallel"),
        name="ssm_matrices",
    )(*args)


def _prepare_params(w_mod, b_mod, w_in, w_uq, w_ukv, w_attn_o, w_conv_o, w_glu, w_ssm_o, w_out, w_ff1, w_ff2,
                    q_lora, kv_lora, conv_w, ssm_w):
    ly = w_in.shape[0]
    swap = _rope_swap_index()
    o = [0, q_lora, q_lora + kv_lora, q_lora + kv_lora + ROPE_DIM]
    o += [o[3] + conv_w, o[3] + 2 * conv_w, o[3] + 2 * conv_w + ssm_w]
    cols = lambda a, b: w_in[:, :, a:b].astype(BF16)
    kpe = cols(o[2], o[3])
    qk = NOPE_DIM + ROPE_DIM
    uq = w_uq.astype(BF16).reshape(ly, q_lora, N_HEADS, qk)
    pe = uq[..., NOPE_DIM:]
    ukv = w_ukv.astype(BF16).reshape(ly, kv_lora, N_HEADS, NOPE_DIM + V_DIM)
    return dict(
        w_mod=w_mod, b_mod=b_mod[:, None, :],
        w_a=jnp.concatenate([cols(0, o[2]), kpe, kpe[:, :, swap]], axis=2),
        w_branch=cols(o[3], w_in.shape[2]), branch_cols=[c - o[3] for c in o[3:]],
        w_uq=jnp.concatenate([uq[..., :NOPE_DIM], pe, pe[..., swap]], axis=-1).reshape(ly, q_lora, N_HEADS * HEAD_W),
        w_uk=ukv[..., :NOPE_DIM].reshape(ly, kv_lora, N_HEADS * NOPE_DIM),
        w_uv=ukv[..., NOPE_DIM:].reshape(ly, kv_lora, N_HEADS * V_DIM),
        w_attn_o=w_attn_o.astype(BF16), w_conv_o=w_conv_o.astype(BF16), w_glu=w_glu.astype(BF16),
        w_ssm_o=w_ssm_o.astype(BF16), w_out=w_out.astype(BF16), w_ff1=w_ff1.astype(BF16),
        w_ff2=w_ff2.astype(BF16),
    )


def _block(x, mod, pp, vec, ssm_m, layer, batch, seq_len, rope_tabs, cache):
    shift1, scale1, gate1, shift2, scale2, gate2 = mod
    t = x.shape[0]
    cc, ss = rope_tabs
    w_br = pp["w_branch"]
    c_conv_a, c_conv_b, c_ssm, c_gate = pp["branch_cols"]
    n_gate = w_br.shape[2] - c_gate

    q, ckv, kpe2, h = _attn_in(x, vec["g_mix_pre"], scale1, shift1, pp["w_a"], vec["g_q"], vec["g_kv"],
                               pp["w_uq"], cc, ss, layer, seq_len)
    k_new, v_new = _kv_build(ckv, kpe2, cc, ss, pp["w_uk"], pp["w_uv"], layer)
    shape3 = lambda a, b: a.reshape(b, a.shape[0] // b, a.shape[1])
    kvs = [(shape3(k_new, batch), shape3(v_new, batch))]
    if cache is not None:
        ckv_ctx, kpe_ctx, icc, iss = cache
        k_ctx, v_ctx = _kv_build(ckv_ctx, kpe_ctx, icc, iss, pp["w_uk"], pp["w_uv"], layer)
        kvs.append((shape3(k_ctx, batch), shape3(v_ctx, batch)))
    heads_per_step = min(N_HEADS, 8)
    ao = _attention(shape3(q, batch), kvs, heads_per_step).reshape(t, N_HEADS * V_DIM)

    z = _proj(h, w_br, [c_conv_a, c_conv_b], c_conv_b - c_conv_a, layer, "glu")
    cv = _conv_branch(z, vec["conv_w"], vec["conv_b"], layer, batch, seq_len)

    u = _proj(h, w_br, [c_ssm], c_gate - c_ssm, layer, "none", out_dtype=F32)
    y, fin = _ssm_scan(u, ssm_m["h0"], ssm_m["t"], ssm_m["p"], ssm_m["rf"], ssm_m["rb"], ssm_m["a"], layer, batch)
    sa = _ssm_post(y, pp["w_glu"], vec["b_glu"], layer)

    gates = _proj(h, w_br, [c_gate], n_gate, layer, "sigmoid", tn=2048)
    x = _branch_out(ao, cv, sa, gates, x, pp["w_attn_o"], pp["w_conv_o"], pp["w_ssm_o"], pp["w_out"],
                    vec["conv_ln_g"], vec["conv_ln_b"], vec["g_mix_post"], gate1, layer, seq_len)
    x = _mlp(x, vec["g_mlp_pre"], scale2, shift2, pp["w_ff1"], pp["w_ff2"], vec["g_mlp_post"], gate2, layer,
             seq_len)
    return x, ckv, kpe2[:, :ROPE_DIM], fin


def kernel(x_prompt, x_sample, cache_ckv, cache_kpe, state_ssm_re, state_ssm_im, c, c_ctx, w_mod, b_mod, g_mix_pre, g_mix_post, g_mlp_pre, g_mlp_post, w_in, g_q, w_uq, g_kv, w_ukv, w_attn_o, conv_w, conv_b, conv_ln_g, conv_ln_b, w_conv_o, ssm_a_re, ssm_a_im, ssm_log_dt, ssm_b_re, ssm_b_im, ssm_c_re, ssm_c_im, ssm_d, w_glu, b_glu, w_ssm_o, w_out, w_ff1, w_ff2):
    bp, lp, d = x_prompt.shape
    bs, ls, _ = x_sample.shape
    depth = w_in.shape[0]
    past = cache_ckv.shape[2]
    q_lora, kv_lora = g_q.shape[1], g_kv.shape[1]
    conv_ch, ssm_ch = conv_w.shape[2], ssm_d.shape[1]
    n_groups = ssm_ch // SSM_GROUP_CH

    pp = _prepare_params(w_mod, b_mod, w_in, w_uq, w_ukv, w_attn_o, w_conv_o, w_glu, w_ssm_o, w_out, w_ff1,
                         w_ff2, q_lora, kv_lora, conv_ch, ssm_ch)
    row = lambda a: a[:, None, :]
    vec = dict(g_mix_pre=row(g_mix_pre), g_mix_post=row(g_mix_post), g_mlp_pre=row(g_mlp_pre),
               g_mlp_post=row(g_mlp_post), g_q=row(g_q), g_kv=row(g_kv), conv_w=conv_w, conv_b=row(conv_b),
               conv_ln_g=row(conv_ln_g), conv_ln_b=row(conv_ln_b), b_glu=row(b_glu))
    t_m, p_m, rf_m, rb_m, a_m = _ssm_matrices(ssm_a_re, ssm_a_im, ssm_log_dt, ssm_b_re, ssm_b_im, ssm_c_re, ssm_c_im,
                                       ssm_d)

    n_cond = 1 + bs
    cond = jnp.concatenate([c_ctx[None, :], c, jnp.zeros((-n_cond % 8, d), F32)], axis=0)
    mod_all = _modulation(cond, pp["w_mod"], pp["b_mod"])

    rope_s = _rope_tables(ls)
    ident_p = _identity_tables(min(lp, 256))
    ident_c = _identity_tables(min(past, 256))
    zero_h0 = jnp.zeros((n_groups, bp, 4 * SSM_STATE), F32)

    xp = x_prompt.reshape(bp * lp, d)
    xs = x_sample.reshape(bs * ls, d)
    ckvs, kpes, fins = [], [], []
    for l in range(depth):
        mods = jnp.split(mod_all[l], 6, axis=-1)
        mod_ctx = [m[0:1, None, :] for m in mods]
        mod_lat = [m[1:n_cond, None, :] for m in mods]
        ssm_p = dict(t=t_m, p=p_m, rf=rf_m, rb=rb_m, a=a_m, h0=zero_h0)
        xp, ckv, kpe, fin = _block(xp, mod_ctx, pp, vec, ssm_p, l, bp, lp, ident_p, None)
        ckvs.append(ckv.reshape(bp, lp, kv_lora))
        kpes.append(kpe.reshape(bp, lp, ROPE_DIM))
        fins.append(fin)
        sr, si = state_ssm_re[:, l], state_ssm_im[:, l]
        h0 = jnp.concatenate([sr[:, 0], sr[:, 1], si[:, 0], si[:, 1]], axis=-1).transpose(1, 0, 2)
        kpe_ctx = jnp.pad(cache_kpe[:, l].reshape(bs * past, ROPE_DIM), ((0, 0), (0, ROPE_DIM)))
        cache = (cache_ckv[:, l].reshape(bs * past, kv_lora), kpe_ctx) + ident_c
        ssm_s = dict(t=t_m, p=p_m, rf=rf_m, rb=rb_m, a=a_m, h0=h0)
        xs = _block(xs, mod_lat, pp, vec, ssm_s, l, bs, ls, rope_s, cache)[0]

    new_ckv = jnp.stack(ckvs, axis=1)
    new_kpe = jnp.stack(kpes, axis=1)
    fin = jnp.stack(fins, axis=0)
    fin = fin.reshape(depth, n_groups, bp, 2, 2, SSM_STATE).transpose(3, 2, 0, 4, 1, 5)
    return (xp.reshape(bp, lp, d), xs.reshape(bs, ls, d), new_ckv, new_kpe, fin[0], fin[1])
```

```python
import functools
import math

import jax
import jax.numpy as jnp
from jax import lax
from jax.experimental import pallas as pl
from jax.experimental.pallas import tpu as pltpu

N_HEADS = 16
NOPE_DIM = 128
ROPE_DIM = 64
V_DIM = 128
GRID_W = 64
ROPE_BASE = 10000.0
SSM_GROUP_CH = 16
SSM_STATE = 64
EPS = 1e-6

HEAD_W = NOPE_DIM + 2 * ROPE_DIM
_Q_SCALE = math.log2(math.e) / math.sqrt(NOPE_DIM + ROPE_DIM)
SSM_CHUNK = 16
V7X_VMEM_LIMIT_BYTES = 56 * 1024 * 1024
_LANES = 128

F32 = jnp.float32
BF16 = jnp.bfloat16


def _cparams(*sem):
    return pltpu.CompilerParams(dimension_semantics=sem, vmem_limit_bytes=V7X_VMEM_LIMIT_BYTES)


def _tile(n, target):
    if n <= target:
        return n
    for t in range(target, 7, -1):
        if n % t == 0 and t % 8 == 0:
            return t
    return n


def _rms(x, g):
    return x * lax.rsqrt(jnp.mean(x * x, axis=-1, keepdims=True) + EPS) * g


def _rmsmod(x, g, scale, shift):
    return _rms(x, g) * (1.0 + scale) + shift


def _mod_kernel(c_ref, w_ref, b_ref, o_ref):
    h = jax.nn.silu(c_ref[...]).astype(BF16)
    o_ref[...] = jnp.dot(h, w_ref[...].astype(BF16), preferred_element_type=F32) + b_ref[...]


def _modulation(cond, w_mod, b_mod):
    ly, d, n = w_mod.shape
    r = cond.shape[0]
    tn = _tile(n, 1536)
    return pl.pallas_call(
        _mod_kernel,
        grid=(ly, n // tn),
        in_specs=[
            pl.BlockSpec((r, d), lambda l, j: (0, 0)),
            pl.BlockSpec((None, d, tn), lambda l, j: (l, 0, j)),
            pl.BlockSpec((None, 1, tn), lambda l, j: (l, 0, j)),
        ],
        out_specs=pl.BlockSpec((None, r, tn), lambda l, j: (l, 0, j)),
        out_shape=jax.ShapeDtypeStruct((ly, r, n), F32),
        compiler_params=_cparams("parallel", "parallel"),
        name="modulation",
    )(cond, w_mod, b_mod)


def _proj_kernel(h_ref, *rest, n_w, epilogue):
    w_refs, o_ref = rest[:n_w], rest[n_w]
    h = h_ref[...]
    a = jnp.dot(h, w_refs[0][...], preferred_element_type=F32)
    if epilogue == "glu":
        a = a * jax.nn.sigmoid(jnp.dot(h, w_refs[1][...], preferred_element_type=F32))
    elif epilogue == "sigmoid":
        a = jax.nn.sigmoid(a)
    o_ref[...] = a.astype(o_ref.dtype)


def _proj(h, w, col_starts, n, layer, epilogue, out_dtype=BF16, tm=1024, tn=1024):
    t, d = h.shape
    tm = _tile(t, tm)
    tn = next(c for c in range(min(n, tn // len(col_starts)), 0, -_LANES)
              if n % c == 0 and all(s % c == 0 for s in col_starts))
    w_specs = [pl.BlockSpec((None, d, tn), functools.partial(lambda i, j, off: (layer, 0, off + j), off=c // tn))
               for c in col_starts]
    return pl.pallas_call(
        functools.partial(_proj_kernel, n_w=len(col_starts), epilogue=epilogue),
        grid=(t // tm, n // tn),
        in_specs=[pl.BlockSpec((tm, d), lambda i, j: (i, 0))] + w_specs,
        out_specs=pl.BlockSpec((tm, tn), lambda i, j: (i, j)),
        out_shape=jax.ShapeDtypeStruct((t, n), out_dtype),
        compiler_params=_cparams("parallel", "parallel"),
        name="proj_" + epilogue,
    )(h, *([w] * len(col_starts)))


def _attn_in_kernel(x_ref, g_ref, sc_ref, sh_ref, wa_ref, gq_ref, gkv_ref, wuq_ref, cc_ref, ss_ref,
                    q_ref, ckv_ref, kpe_ref, h_ref, *, q_lora, kv_lora):
    h = _rmsmod(x_ref[...], g_ref[...], sc_ref[0], sh_ref[0]).astype(BF16)
    h_ref[...] = h
    a = jnp.dot(h, wa_ref[...], preferred_element_type=F32)
    cq = _rms(a[:, :q_lora], gq_ref[...]).astype(BF16)
    ckv_ref[...] = _rms(a[:, q_lora:q_lora + kv_lora], gkv_ref[...])
    kpe_ref[...] = a[:, q_lora + kv_lora:]
    q = jnp.dot(cq, wuq_ref[...], preferred_element_type=F32) * _Q_SCALE
    cc, ss = cc_ref[...], ss_ref[...]
    for hd in range(N_HEADS):
        lo = hd * HEAD_W
        q_ref[:, lo:lo + NOPE_DIM] = q[:, lo:lo + NOPE_DIM].astype(BF16)
        v = q[:, lo + NOPE_DIM:lo + HEAD_W]
        q_ref[:, lo + NOPE_DIM:lo + HEAD_W] = (v * cc + pltpu.roll(v, ROPE_DIM, 1) * ss).astype(BF16)


def _attn_in(x, g, scale, shift, w_a, g_q, g_kv, w_uq, cc, ss, layer, seq_len, tm=256):
    t, d = x.shape
    na = w_a.shape[2]
    q_lora, kv_lora = g_q.shape[2], g_kv.shape[2]
    nq = w_uq.shape[2]
    tm = _tile(seq_len, tm)
    per_seq = seq_len // tm
    nb = scale.shape[0]
    bidx = (lambda i: i // per_seq) if nb > 1 else (lambda i: 0)
    ntab = cc.shape[0] // tm
    lsel = lambda i: (layer, 0, 0)
    return pl.pallas_call(
        functools.partial(_attn_in_kernel, q_lora=q_lora, kv_lora=kv_lora),
        grid=(t // tm,),
        in_specs=[
            pl.BlockSpec((tm, d), lambda i: (i, 0)),
            pl.BlockSpec((None, 1, d), lsel),
            pl.BlockSpec((1, 1, d), lambda i: (bidx(i), 0, 0)),
            pl.BlockSpec((1, 1, d), lambda i: (bidx(i), 0, 0)),
            pl.BlockSpec((None, d, na), lsel),
            pl.BlockSpec((None, 1, q_lora), lsel),
            pl.BlockSpec((None, 1, kv_lora), lsel),
            pl.BlockSpec((None, q_lora, nq), lsel),
            pl.BlockSpec((tm, 2 * ROPE_DIM), lambda i: (i % ntab, 0)),
            pl.BlockSpec((tm, 2 * ROPE_DIM), lambda i: (i % ntab, 0)),
        ],
        out_specs=[
            pl.BlockSpec((tm, nq), lambda i: (i, 0)),
            pl.BlockSpec((tm, kv_lora), lambda i: (i, 0)),
            pl.BlockSpec((tm, 2 * ROPE_DIM), lambda i: (i, 0)),
            pl.BlockSpec((tm, d), lambda i: (i, 0)),
        ],
        out_shape=[
            jax.ShapeDtypeStruct((t, nq), BF16),
            jax.ShapeDtypeStruct((t, kv_lora), F32),
            jax.ShapeDtypeStruct((t, 2 * ROPE_DIM), F32),
            jax.ShapeDtypeStruct((t, d), BF16),
        ],
        compiler_params=_cparams("parallel"),
        name="attn_in",
    )(x, g, scale, shift, w_a, g_q, g_kv, w_uq, cc, ss)


_NT_DIMS = (((1,), (1,)), ((), ()))


def _kv_build_kernel(ckv_ref, kpe_ref, cc_ref, ss_ref, wuk_ref, wuv_ref, k_ref, v_ref):
    c = ckv_ref[...].astype(BF16)
    kn = jnp.dot(c, wuk_ref[...], preferred_element_type=F32)
    v_ref[...] = jnp.dot(c, wuv_ref[...], preferred_element_type=F32).astype(BF16)
    kp = kpe_ref[...]
    kp = (kp * cc_ref[...] + pltpu.roll(kp, ROPE_DIM, 1) * ss_ref[...]).astype(BF16)
    for hd in range(N_HEADS):
        k_ref[:, hd * HEAD_W:hd * HEAD_W + NOPE_DIM] = kn[:, hd * NOPE_DIM:(hd + 1) * NOPE_DIM].astype(BF16)
        k_ref[:, hd * HEAD_W + NOPE_DIM:(hd + 1) * HEAD_W] = kp


def _kv_build(ckv, kpe2, cc, ss, w_uk, w_uv, layer, tm=256):
    t, kvl = ckv.shape
    tm = _tile(cc.shape[0], tm)
    ntab = cc.shape[0] // tm
    nk, nv = N_HEADS * HEAD_W, w_uv.shape[2]
    lsel = lambda i: (layer, 0, 0)
    return pl.pallas_call(
        _kv_build_kernel,
        grid=(t // tm,),
        in_specs=[
            pl.BlockSpec((tm, kvl), lambda i: (i, 0)),
            pl.BlockSpec((tm, 2 * ROPE_DIM), lambda i: (i, 0)),
            pl.BlockSpec((tm, 2 * ROPE_DIM), lambda i: (i % ntab, 0)),
            pl.BlockSpec((tm, 2 * ROPE_DIM), lambda i: (i % ntab, 0)),
            pl.BlockSpec((None, kvl, w_uk.shape[2]), lsel),
            pl.BlockSpec((None, kvl, nv), lsel),
        ],
        out_specs=[pl.BlockSpec((tm, nk), lambda i: (i, 0)), pl.BlockSpec((tm, nv), lambda i: (i, 0))],
        out_shape=[jax.ShapeDtypeStruct((t, nk), BF16), jax.ShapeDtypeStruct((t, nv), BF16)],
        compiler_params=_cparams("parallel"),
        name="kv_build",
    )(ckv, kpe2, cc, ss, w_uk, w_uv)


def _attn_kernel(q_ref, *rest, n_src, heads):
    kv_refs, o_ref = rest[:2 * n_src], rest[2 * n_src]
    for hd in range(heads):
        q = q_ref[:, hd * HEAD_W:(hd + 1) * HEAD_W]
        s = [lax.dot_general(q, kv_refs[2 * i][:, hd * HEAD_W:(hd + 1) * HEAD_W], _NT_DIMS,
                             preferred_element_type=F32) for i in range(n_src)]
        m = jnp.max(s[0], axis=-1, keepdims=True)
        for si in s[1:]:
            m = jnp.maximum(m, jnp.max(si, axis=-1, keepdims=True))
        acc, den = None, None
        for i in range(n_src):
            p = jnp.exp2(s[i] - m)
            li = jnp.sum(p, axis=-1, keepdims=True)
            oi = jnp.dot(p.astype(BF16), kv_refs[2 * i + 1][:, hd * V_DIM:(hd + 1) * V_DIM],
                         preferred_element_type=F32)
            acc = oi if acc is None else acc + oi
            den = li if den is None else den + li
        o_ref[:, hd * V_DIM:(hd + 1) * V_DIM] = (acc / den).astype(BF16)


def _attention(q, kvs, heads_per_step, tq=256):
    b, sq, _ = q.shape
    tq = _tile(sq, tq)
    hb = heads_per_step
    in_specs = [pl.BlockSpec((None, tq, hb * HEAD_W), lambda bi, hi, qi: (bi, qi, hi))]
    args = [q]
    for k, v in kvs:
        sk = k.shape[1]
        in_specs.append(pl.BlockSpec((None, sk, hb * HEAD_W), lambda bi, hi, qi: (bi, 0, hi)))
        in_specs.append(pl.BlockSpec((None, sk, hb * V_DIM), lambda bi, hi, qi: (bi, 0, hi)))
        args += [k, v]
    return pl.pallas_call(
        functools.partial(_attn_kernel, n_src=len(kvs), heads=hb),
        grid=(b, N_HEADS // hb, sq // tq),
        in_specs=in_specs,
        out_specs=pl.BlockSpec((None, tq, hb * V_DIM), lambda bi, hi, qi: (bi, qi, hi)),
        out_shape=jax.ShapeDtypeStruct((b, sq, N_HEADS * V_DIM), BF16),
        compiler_params=_cparams("parallel", "parallel", "parallel"),
        name="attention",
    )(*args)


_CONV_ROWS = 64
_CONV_LANES = 256
_CONV_HALO = 16
_SUBLANES = 8


def _conv_kernel(z_ref, w_ref, b_ref, o_ref, zs_ref, wb_ref, *, seq_len, taps):
    lanes = z_ref.shape[-1]
    pad = (taps - 1) // 2
    halo = jnp.zeros((_CONV_HALO, lanes), F32)
    zp = jnp.concatenate([halo, z_ref[...].astype(F32), halo], axis=0)
    span = seq_len + 2 * _CONV_HALO - _SUBLANES
    for j in range(_SUBLANES):
        zs_ref[j, 0:span, :] = zp[j:j + span, :]
    for k in range(taps):
        wb_ref[k] = jnp.broadcast_to(w_ref[k:k + 1, :], (_SUBLANES, lanes))
    bias = jnp.broadcast_to(b_ref[...], (_SUBLANES, lanes))
    n_sub = _CONV_ROWS // _SUBLANES

    def conv_step(it, carry):
        r0 = pl.multiple_of(it * _CONV_ROWS, _CONV_ROWS)
        acc = [bias] * n_sub
        for k in range(taps):
            off = _CONV_HALO - pad + k
            j, base = off % _SUBLANES, off - off % _SUBLANES
            win = zs_ref[j, pl.ds(r0 + base, _CONV_ROWS), :]
            wk = wb_ref[k]
            acc = [acc[p] + wk * win[p * _SUBLANES:(p + 1) * _SUBLANES, :] for p in range(n_sub)]
        o_ref[pl.ds(r0, _CONV_ROWS), :] = jnp.concatenate(acc, axis=0).astype(BF16)
        return carry

    lax.fori_loop(0, seq_len // _CONV_ROWS, conv_step, 0)


def _conv_branch(z, conv_w, conv_b, layer, batch, seq_len):
    t, ch = z.shape
    taps = conv_w.shape[1]
    lanes = _tile(ch, _CONV_LANES)
    return pl.pallas_call(
        functools.partial(_conv_kernel, seq_len=seq_len, taps=taps),
        grid=(batch, ch // lanes),
        in_specs=[
            pl.BlockSpec((seq_len, lanes), lambda b, c: (b, c)),
            pl.BlockSpec((None, taps, lanes), lambda b, c: (layer, 0, c)),
            pl.BlockSpec((None, 1, lanes), lambda b, c: (layer, 0, c)),
        ],
        out_specs=pl.BlockSpec((seq_len, lanes), lambda b, c: (b, c)),
        out_shape=jax.ShapeDtypeStruct((t, ch), BF16),
        scratch_shapes=[pltpu.VMEM((_SUBLANES, seq_len + 2 * _CONV_HALO, lanes), F32),
                        pltpu.VMEM((taps, _SUBLANES, lanes), F32)],
        compiler_params=_cparams("parallel", "parallel"),
        name="depthwise_conv",
    )(z, conv_w, conv_b)


_SSM_TILE_GROUPS = _LANES // SSM_GROUP_CH
_SSM_MOVE_ROWS = 64
_SSM_SCAN_GROUPS = 2


def _ssm_kernel(x_ref, h0_ref, t_ref, p_ref, rf_ref, rb_ref, a_ref, y_ref, fin_ref,
                u_scr, y_scr, ds_scr, sf_scr, sb_scr, *, n_chunks, batch):
    m = n_chunks * batch
    ng, q, gc = _SSM_TILE_GROUPS, SSM_CHUNK, SSM_GROUP_CH
    half = 2 * SSM_STATE
    rr = min(_SSM_MOVE_ROWS, m)
    blk = lax.broadcasted_iota(jnp.int32, (rr, _LANES), 1) // gc
    is_fwd = lax.broadcasted_iota(jnp.int32, (batch, half), 1) < SSM_STATE

    def block_transpose(vs):
        vs = list(vs)
        d = ng // 2
        while d >= 1:
            odd = (blk // d) % 2 == 1
            nxt = list(vs)
            for a in range(ng):
                if (a // d) % 2 == 0:
                    lo, hi = vs[a], vs[a + d]
                    nxt[a] = jnp.where(odd, pltpu.roll(hi, d * gc, 1), lo)
                    nxt[a + d] = jnp.where(odd, hi, pltpu.roll(lo, _LANES - d * gc, 1))
            vs = nxt
            d //= 2
        return vs

    def gather_in(it, carry):
        r0 = pl.multiple_of(it * rr, rr)
        halves = [block_transpose([x_ref[pl.ds(r0, rr), h * ng + j, :] for j in range(ng)])
                  for h in range(q // ng)]
        for g in range(ng):
            u_scr[g, pl.ds(r0, rr), :] = jnp.concatenate([hv[g] for hv in halves], axis=1).astype(BF16)
        return carry

    lax.fori_loop(0, m // rr, gather_in, 0)

    def group_pair(it, carry):
        gs = [_SSM_SCAN_GROUPS * it + k for k in range(_SSM_SCAN_GROUPS)]
        a_re, a_im, s_re, s_im = [], [], [], []
        for k, g in enumerate(gs):
            ds_scr[k] = jnp.dot(u_scr[g], p_ref[g], preferred_element_type=F32).reshape(batch, n_chunks, 2 * half)
            a, h0 = a_ref[g], h0_ref[g]
            a_re.append(a[:, :half]); a_im.append(a[:, half:])
            s_re.append(h0[:, :half]); s_im.append(h0[:, half:])
        for c in range(n_chunks):
            cb = n_chunks - 1 - c
            for k in range(_SSM_SCAN_GROUPS):
                state = jnp.concatenate([s_re[k], s_im[k]], axis=1)
                sf_scr[k, c] = state
                sb_scr[k, cb] = state
                d_re = jnp.where(is_fwd, ds_scr[k, :, c, :half], ds_scr[k, :, cb, :half])
                d_im = jnp.where(is_fwd, ds_scr[k, :, c, half:], ds_scr[k, :, cb, half:])
                s_re[k], s_im[k] = (a_re[k] * s_re[k] - a_im[k] * s_im[k] + d_re,
                                    a_re[k] * s_im[k] + a_im[k] * s_re[k] + d_im)
        for k, g in enumerate(gs):
            fin_ref[g] = jnp.concatenate([s_re[k], s_im[k]], axis=1)
            by_batch = lambda scr: jnp.concatenate([scr[k, :, b, :] for b in range(batch)], axis=0).astype(BF16)
            y = jnp.dot(u_scr[g], t_ref[g], preferred_element_type=F32)
            y = y + jnp.dot(by_batch(sf_scr), rf_ref[g], preferred_element_type=F32)
            y = y + jnp.dot(by_batch(sb_scr), rb_ref[g], preferred_element_type=F32)
            y_scr[g] = y
        return carry

    lax.fori_loop(0, ng // _SSM_SCAN_GROUPS, group_pair, 0)

    def scatter_out(it, carry):
        r0 = pl.multiple_of(it * rr, rr)
        ys = [y_scr[g, pl.ds(r0, rr), :] for g in range(ng)]
        for h in range(q // ng):
            toks = block_transpose([y[:, h * _LANES:(h + 1) * _LANES] for y in ys])
            for j in range(ng):
                y_ref[pl.ds(r0, rr), h * ng + j, :] = toks[j]
        return carry

    lax.fori_loop(0, m // rr, scatter_out, 0)


def _ssm_scan(u, h0, t_m, p_m, rf_m, rb_m, a_m, layer, batch):
    t, ch = u.shape
    m = t // SSM_CHUNK
    n_chunks = m // batch
    ng, w = _SSM_TILE_GROUPS, SSM_CHUNK * SSM_GROUP_CH
    g_all = ch // SSM_GROUP_CH
    tok = pl.BlockSpec((m, SSM_CHUNK, _LANES), lambda i: (0, 0, i))
    wsel = lambda i: (layer, i, 0, 0)
    mat = pl.BlockSpec((None, ng, w, w), wsel)
    st = pl.BlockSpec((ng, batch, w), lambda i: (i, 0, 0))
    y, fin = pl.pallas_call(
        functools.partial(_ssm_kernel, n_chunks=n_chunks, batch=batch),
        grid=(g_all // ng,),
        in_specs=[tok, st, mat, mat, mat, mat, pl.BlockSpec((None, ng, 1, w), wsel)],
        out_specs=[tok, st],
        out_shape=[jax.ShapeDtypeStruct((m, SSM_CHUNK, ch), F32), jax.ShapeDtypeStruct((g_all, batch, w), F32)],
        scratch_shapes=[pltpu.VMEM((ng, m, w), BF16), pltpu.VMEM((ng, m, w), F32),
                        pltpu.VMEM((_SSM_SCAN_GROUPS, batch, n_chunks, w), F32),
                        pltpu.VMEM((_SSM_SCAN_GROUPS, n_chunks, batch, w), F32),
                        pltpu.VMEM((_SSM_SCAN_GROUPS, n_chunks, batch, w), F32)],
        compiler_params=_cparams("parallel"),
        name="ssm_chunk_scan",
    )(u.reshape(m, SSM_CHUNK, ch), h0, t_m, p_m, rf_m, rb_m, a_m)
    return y.reshape(t, ch), fin


def _accumulate_then(acc_ref, part, k, nk, finish):
    if nk == 1:
        finish(part)
        return

    @pl.when(k == 0)
    def _():
        acc_ref[...] = part

    @pl.when(jnp.logical_and(k > 0, k < nk - 1))
    def _():
        acc_ref[...] += part

    @pl.when(k == nk - 1)
    def _():
        finish(acc_ref[...] + part)


def _branch_out_kernel(ao_ref, cv_ref, y_ref, gates_ref, wa_ref, wc_ref, ws_ref, wo_ref, wg_ref, bg_ref,
                       lg_ref, lb_ref, g_ref, gate_ref, x_ref, o_ref):
    d = x_ref.shape[-1]
    gl = jax.nn.gelu(y_ref[...])
    sg = jnp.dot(gl.astype(BF16), wg_ref[...], preferred_element_type=F32) + bg_ref[...]
    sa = (gl * jax.nn.sigmoid(sg)).astype(BF16)
    v = cv_ref[...].astype(F32)
    vc = v - jnp.mean(v, axis=-1, keepdims=True)
    var = jnp.mean(vc * vc, axis=-1, keepdims=True)
    ca = jax.nn.silu(vc * lax.rsqrt(var + EPS) * lg_ref[...] + lb_ref[...]).astype(BF16)
    mix = gates_ref[:, 0:d].astype(F32) * jnp.dot(ao_ref[...], wa_ref[...], preferred_element_type=F32)
    mix = mix + gates_ref[:, d:2 * d].astype(F32) * jnp.dot(ca, wc_ref[...], preferred_element_type=F32)
    mix = mix + gates_ref[:, 2 * d:].astype(F32) * jnp.dot(sa, ws_ref[...], preferred_element_type=F32)
    total = jnp.dot(mix.astype(BF16), wo_ref[...], preferred_element_type=F32)
    o_ref[...] = x_ref[...] + gate_ref[0] * _rms(total, g_ref[...])


def _branch_out(ao, cv, y, gates, x, w_attn_o, w_conv_o, w_ssm_o, w_out, w_glu, b_glu, ln_g, ln_b, g_post, gate,
                layer, seq_len, tm=256):
    t, d = x.shape
    nb = gate.shape[0]
    tm = _tile(t if nb == 1 else seq_len, tm)
    per_seq = seq_len // tm
    bidx = (lambda i: i // per_seq) if nb > 1 else (lambda i: 0)
    act = lambda a: pl.BlockSpec((tm, a.shape[1]), lambda i: (i, 0))
    fixed = lambda a: pl.BlockSpec((None,) + a.shape[1:], lambda i: (layer, 0, 0), pipeline_mode=pl.Buffered(1))
    return pl.pallas_call(
        _branch_out_kernel,
        grid=(t // tm,),
        in_specs=[act(ao), act(cv), act(y), act(gates),
                  fixed(w_attn_o), fixed(w_conv_o), fixed(w_ssm_o), fixed(w_out), fixed(w_glu), fixed(b_glu),
                  fixed(ln_g), fixed(ln_b), fixed(g_post),
                  pl.BlockSpec((1, 1, d), lambda i: (bidx(i), 0, 0)), act(x)],
        out_specs=act(x),
        out_shape=jax.ShapeDtypeStruct((t, d), F32),
        compiler_params=_cparams("parallel"),
        name="branch_out",
    )(ao, cv, y, gates, w_attn_o, w_conv_o, w_ssm_o, w_out, w_glu, b_glu, ln_g, ln_b, g_post, gate, x)


def _mlp_kernel(x_ref, gpre_ref, sc_ref, sh_ref, w1_ref, w2_ref, gpost_ref, gate_ref, o_ref, h_ref, acc_ref, *,
                nk):
    k = pl.program_id(1)

    @pl.when(k == 0)
    def _():
        h_ref[...] = _rmsmod(x_ref[...], gpre_ref[...], sc_ref[0], sh_ref[0]).astype(BF16)

    f = jnp.square(jnp.maximum(jnp.dot(h_ref[...], w1_ref[...], preferred_element_type=F32), 0.0))
    part = jnp.dot(f.astype(BF16), w2_ref[...], preferred_element_type=F32)

    def finish(total):
        o_ref[...] = x_ref[...] + gate_ref[0] * _rms(total, gpost_ref[...])

    _accumulate_then(acc_ref, part, k, nk, finish)


def _mlp(x, g_pre, scale, shift, w1, w2, g_post, gate, layer, seq_len, tm=512, tf=1024):
    t, d = x.shape
    ff = w1.shape[2]
    nb = gate.shape[0]
    tm, tf = _tile(t if nb == 1 else seq_len, tm), _tile(ff, tf)
    nk = ff // tf
    per_seq = seq_len // tm
    bidx = (lambda i: i // per_seq) if nb > 1 else (lambda i: 0)
    vec = pl.BlockSpec((None, 1, d), lambda i, k: (layer, 0, 0))
    mod = pl.BlockSpec((1, 1, d), lambda i, k: (bidx(i), 0, 0))
    return pl.pallas_call(
        functools.partial(_mlp_kernel, nk=nk),
        grid=(t // tm, nk),
        in_specs=[pl.BlockSpec((tm, d), lambda i, k: (i, 0)), vec, mod, mod,
                  pl.BlockSpec((None, d, tf), lambda i, k: (layer, 0, k)),
                  pl.BlockSpec((None, tf, d), lambda i, k: (layer, k, 0)),
                  vec, mod],
        out_specs=pl.BlockSpec((tm, d), lambda i, k: (i, 0)),
        out_shape=jax.ShapeDtypeStruct((t, d), F32),
        scratch_shapes=[pltpu.VMEM((tm, d), BF16), pltpu.VMEM((tm, d), F32)],
        compiler_params=_cparams("parallel", "arbitrary"),
        name="mlp_relu2",
    )(x, g_pre, scale, shift, w1, w2, g_post, gate)


def _rope_swap_index():
    q = ROPE_DIM // 4
    return jnp.concatenate([jnp.arange(q, 2 * q), jnp.arange(0, q), jnp.arange(3 * q, 4 * q), jnp.arange(2 * q, 3 * q)])


def _rope_tables(n_tok):
    pairs = ROPE_DIM // 4
    pos = jnp.arange(n_tok)
    row = (pos // GRID_W).astype(F32)
    col = (pos % GRID_W).astype(F32)
    inv = ROPE_BASE ** (-jnp.arange(pairs, dtype=F32) / pairs)
    ang = jnp.stack([row[:, None] * inv, col[:, None] * inv], axis=1)
    cos, sin = jnp.cos(ang), jnp.sin(ang)
    cc = jnp.stack([cos, cos], axis=2).reshape(n_tok, ROPE_DIM)
    ss = jnp.stack([-sin, sin], axis=2).reshape(n_tok, ROPE_DIM)
    z = jnp.zeros((n_tok, ROPE_DIM), F32)
    return jnp.concatenate([cc, z], axis=1), jnp.concatenate([ss, z], axis=1)


def _identity_tables(n_tok):
    one = jnp.ones((n_tok, ROPE_DIM), F32)
    z = jnp.zeros((n_tok, ROPE_DIM), F32)
    return jnp.concatenate([one, z], axis=1), jnp.zeros((n_tok, 2 * ROPE_DIM), F32)


_SSM_MAT_GROUPS = 8


def _ssm_mats_kernel(a1_ref, a2_ref, ai_ref, ldt_ref, bx_ref, by_ref, cx_ref, cy_ref, d_ref,
                     t_ref, p_ref, rf_ref, rb_ref, lam_ref, *, groups):
    q, gc, ns = SSM_CHUNK, SSM_GROUP_CH, SSM_STATE
    w = q * gc
    hp = lax.Precision.HIGHEST
    n_pow = 24
    tau = lax.broadcasted_iota(jnp.int32, (n_pow, 2 * ns), 0).astype(F32)
    first = lax.broadcasted_iota(jnp.int32, (1, 2 * ns), 1) < ns
    quarter = jnp.where(first, 0.0, 0.5 * math.pi)
    one_zero = jnp.where(first, 1.0, 0.0)
    lane_w = lax.broadcasted_iota(jnp.int32, (gc, w), 1)
    eye = (lax.broadcasted_iota(jnp.int32, (gc, gc), 0) == lax.broadcasted_iota(jnp.int32, (gc, gc), 1))
    swap = lambda x: pltpu.roll(x, ns, 1)
    nt_dot = lambda x, y: lax.dot_general(x, y, (((1,), (1,)), ((), ())), precision=hp,
                                          preferred_element_type=F32)

    def direction(d, g):
        dt = jnp.exp(ldt_ref[d, g])
        a1, a2 = a1_ref[d, g], a2_ref[d, g]
        pw = jnp.exp(a1 * dt * tau) * jnp.cos(ai_ref[d, g] * dt * tau - quarter)
        pws = swap(pw)
        pw_rr = jnp.where(first, pw, pws)
        pw_ii = jnp.where(first, -pws, pw)
        num = pw[1:2] - one_zero
        den = a1 * a1 + a2 * a2
        f = (num * a1 + swap(num) * a2) / den
        bb = bx_ref[d, g] * f + by_ref[d, g] * swap(f)
        cx, cy = cx_ref[d, g], cy_ref[d, g]
        cl = [cx * pw[t:t + 1] + cy * pws[t:t + 1] for t in range(q + 1)]
        inj = lambda t: bb * pw_rr[t:t + 1] + swap(bb) * pw_ii[t:t + 1]
        return pw, bb, cl, inj

    def interleave(x_f, x_b):
        return jnp.concatenate([jnp.where(first, x_f, swap(x_b)), jnp.where(first, swap(x_f), x_b)], axis=1)

    def one_group(g, carry):
        pw_f, bb_f, cl_f, inj_f = direction(0, g)
        pw_b, bb_b, cl_b, inj_b = direction(1, g)
        krow_f = nt_dot(bb_f, jnp.concatenate(cl_f[:q], axis=0))
        krow_b = nt_dot(bb_b, jnp.concatenate(cl_b[q - 1::-1], axis=0))
        skip = jnp.where(eye, jnp.broadcast_to(d_ref[g], (gc, gc)), 0.0)
        krow_f = krow_f + jnp.concatenate([skip, jnp.zeros((gc, w - gc), F32)], axis=1)
        for s in range(q):
            fwd = krow_f if s == 0 else jnp.where(lane_w >= gc * s, pltpu.roll(krow_f, gc * s, 1), 0.0)
            k = gc * (q - 1 - s)
            bwd = krow_b if k == 0 else jnp.where(lane_w < w - k, pltpu.roll(krow_b, w - k, 1), 0.0)
            t_ref[g, s * gc:(s + 1) * gc, :] = (fwd + bwd).astype(BF16)
            p_ref[g, s * gc:(s + 1) * gc, :] = interleave(inj_f(q - 1 - s), inj_b(s)).astype(BF16)
        tr_f = jnp.concatenate(cl_f[1:q + 1], axis=0).T
        tr_b = jnp.concatenate(cl_b[q:0:-1], axis=0).T
        zero = jnp.zeros((ns, w), F32)
        rf_ref[g] = jnp.concatenate([tr_f[:ns], zero, tr_f[ns:], zero], axis=0).astype(BF16)
        rb_ref[g] = jnp.concatenate([zero, tr_b[:ns], zero, tr_b[ns:]], axis=0).astype(BF16)
        lam_ref[g] = interleave(pw_f[q:q + 1], pw_b[q:q + 1])
        return carry

    lax.fori_loop(0, groups, one_group, 0)


def _ssm_matrices(a_re, a_im, log_dt, b_re, b_im, c_re, c_im, d_skip):
    ly, _, g, p = a_re.shape
    gc, w = SSM_GROUP_CH, SSM_CHUNK * SSM_GROUP_CH
    gb = min(_SSM_MAT_GROUPS, g)
    dup = lambda x, y: jnp.concatenate([x, y], axis=-1)
    row = lambda x: x[:, :, :, None, :]
    bt_re, bt_im = jnp.swapaxes(b_re, -1, -2), jnp.swapaxes(b_im, -1, -2)
    ldt = jnp.broadcast_to(log_dt[..., None], a_re.shape)
    args = [row(dup(a_re, a_re)), row(dup(a_im, -a_im)), row(dup(a_im, a_im)), row(dup(ldt, ldt)),
            dup(bt_re, bt_re), dup(-bt_im, bt_im), dup(c_re, -c_re), dup(-c_im, -c_im),
            d_skip.reshape(ly, g, 1, gc)]
    vec = pl.BlockSpec((None, 2, gb, 1, 2 * p), lambda l, i: (l, 0, i, 0, 0))
    mat = pl.BlockSpec((None, 2, gb, gc, 2 * p), lambda l, i: (l, 0, i, 0, 0))
    out = pl.BlockSpec((None, gb, w, w), lambda l, i: (l, i, 0, 0))
    shape = jax.ShapeDtypeStruct((ly, g, w, w), BF16)
    return pl.pallas_call(
        functools.partial(_ssm_mats_kernel, groups=gb),
        grid=(ly, g // gb),
        in_specs=[vec, vec, vec, vec, mat, mat, mat, mat, pl.BlockSpec((None, gb, 1, gc), lambda l, i: (l, i, 0, 0))],
        out_specs=[out, out, out, out, pl.BlockSpec((None, gb, 1, w), lambda l, i: (l, i, 0, 0))],
        out_shape=[shape, shape, shape, shape, jax.ShapeDtypeStruct((ly, g, 1, w), F32)],
        compiler_params=_cparams("parallel", "parallel"),
        name="ssm_matrices",
    )(*args)


def _prepare_params(w_mod, b_mod, w_in, w_uq, w_ukv, w_attn_o, w_conv_o, w_glu, w_ssm_o, w_out, w_ff1, w_ff2,
                    q_lora, kv_lora, conv_w, ssm_w):
    ly = w_in.shape[0]
    swap = _rope_swap_index()
    o = [0, q_lora, q_lora + kv_lora, q_lora + kv_lora + ROPE_DIM]
    o += [o[3] + conv_w, o[3] + 2 * conv_w, o[3] + 2 * conv_w + ssm_w]
    cols = lambda a, b: w_in[:, :, a:b].astype(BF16)
    kpe = cols(o[2], o[3])
    qk = NOPE_DIM + ROPE_DIM
    uq = w_uq.astype(BF16).reshape(ly, q_lora, N_HEADS, qk)
    pe = uq[..., NOPE_DIM:]
    ukv = w_ukv.astype(BF16).reshape(ly, kv_lora, N_HEADS, NOPE_DIM + V_DIM)
    return dict(
        w_mod=w_mod, b_mod=b_mod[:, None, :],
        w_a=jnp.concatenate([cols(0, o[2]), kpe, kpe[:, :, swap]], axis=2),
        w_branch=cols(o[3], w_in.shape[2]), branch_cols=[c - o[3] for c in o[3:]],
        w_uq=jnp.concatenate([uq[..., :NOPE_DIM], pe, pe[..., swap]], axis=-1).reshape(ly, q_lora, N_HEADS * HEAD_W),
        w_uk=ukv[..., :NOPE_DIM].reshape(ly, kv_lora, N_HEADS * NOPE_DIM),
        w_uv=ukv[..., NOPE_DIM:].reshape(ly, kv_lora, N_HEADS * V_DIM),
        w_attn_o=w_attn_o.astype(BF16), w_conv_o=w_conv_o.astype(BF16), w_glu=w_glu.astype(BF16),
        w_ssm_o=w_ssm_o.astype(BF16), w_out=w_out.astype(BF16), w_ff1=w_ff1.astype(BF16),
        w_ff2=w_ff2.astype(BF16),
    )


def _block(x, mod, pp, vec, ssm_m, layer, batch, seq_len, rope_tabs, cache):
    shift1, scale1, gate1, shift2, scale2, gate2 = mod
    t = x.shape[0]
    cc, ss = rope_tabs
    w_br = pp["w_branch"]
    c_conv_a, c_conv_b, c_ssm, c_gate = pp["branch_cols"]
    n_gate = w_br.shape[2] - c_gate

    q, ckv, kpe2, h = _attn_in(x, vec["g_mix_pre"], scale1, shift1, pp["w_a"], vec["g_q"], vec["g_kv"],
                               pp["w_uq"], cc, ss, layer, seq_len)
    k_new, v_new = _kv_build(ckv, kpe2, cc, ss, pp["w_uk"], pp["w_uv"], layer)
    shape3 = lambda a, b: a.reshape(b, a.shape[0] // b, a.shape[1])
    kvs = [(shape3(k_new, batch), shape3(v_new, batch))]
    if cache is not None:
        ckv_ctx, kpe_ctx, icc, iss = cache
        k_ctx, v_ctx = _kv_build(ckv_ctx, kpe_ctx, icc, iss, pp["w_uk"], pp["w_uv"], layer)
        kvs.append((shape3(k_ctx, batch), shape3(v_ctx, batch)))
    heads_per_step = min(N_HEADS, 8)
    ao = _attention(shape3(q, batch), kvs, heads_per_step).reshape(t, N_HEADS * V_DIM)

    z = _proj(h, w_br, [c_conv_a, c_conv_b], c_conv_b - c_conv_a, layer, "glu")
    cv = _conv_branch(z, vec["conv_w"], vec["conv_b"], layer, batch, seq_len)

    u = _proj(h, w_br, [c_ssm], c_gate - c_ssm, layer, "none", out_dtype=F32)
    y, fin = _ssm_scan(u, ssm_m["h0"], ssm_m["t"], ssm_m["p"], ssm_m["rf"], ssm_m["rb"], ssm_m["a"], layer, batch)

    gates = _proj(h, w_br, [c_gate], n_gate, layer, "sigmoid", tn=2048)
    x = _branch_out(ao, cv, y, gates, x, pp["w_attn_o"], pp["w_conv_o"], pp["w_ssm_o"], pp["w_out"], pp["w_glu"],
                    vec["b_glu"], vec["conv_ln_g"], vec["conv_ln_b"], vec["g_mix_post"], gate1, layer, seq_len)
    x = _mlp(x, vec["g_mlp_pre"], scale2, shift2, pp["w_ff1"], pp["w_ff2"], vec["g_mlp_post"], gate2, layer,
             seq_len)
    return x, ckv, kpe2[:, :ROPE_DIM], fin


def kernel(x_prompt, x_sample, cache_ckv, cache_kpe, state_ssm_re, state_ssm_im, c, c_ctx, w_mod, b_mod, g_mix_pre, g_mix_post, g_mlp_pre, g_mlp_post, w_in, g_q, w_uq, g_kv, w_ukv, w_attn_o, conv_w, conv_b, conv_ln_g, conv_ln_b, w_conv_o, ssm_a_re, ssm_a_im, ssm_log_dt, ssm_b_re, ssm_b_im, ssm_c_re, ssm_c_im, ssm_d, w_glu, b_glu, w_ssm_o, w_out, w_ff1, w_ff2):
    bp, lp, d = x_prompt.shape
    bs, ls, _ = x_sample.shape
    depth = w_in.shape[0]
    past = cache_ckv.shape[2]
    q_lora, kv_lora = g_q.shape[1], g_kv.shape[1]
    conv_ch, ssm_ch = conv_w.shape[2], ssm_d.shape[1]
    n_groups = ssm_ch // SSM_GROUP_CH

    pp = _prepare_params(w_mod, b_mod, w_in, w_uq, w_ukv, w_attn_o, w_conv_o, w_glu, w_ssm_o, w_out, w_ff1,
                         w_ff2, q_lora, kv_lora, conv_ch, ssm_ch)
    row = lambda a: a[:, None, :]
    vec = dict(g_mix_pre=row(g_mix_pre), g_mix_post=row(g_mix_post), g_mlp_pre=row(g_mlp_pre),
               g_mlp_post=row(g_mlp_post), g_q=row(g_q), g_kv=row(g_kv), conv_w=conv_w, conv_b=row(conv_b),
               conv_ln_g=row(conv_ln_g), conv_ln_b=row(conv_ln_b), b_glu=row(b_glu))
    t_m, p_m, rf_m, rb_m, a_m = _ssm_matrices(ssm_a_re, ssm_a_im, ssm_log_dt, ssm_b_re, ssm_b_im, ssm_c_re, ssm_c_im,
                                       ssm_d)

    n_cond = 1 + bs
    cond = jnp.concatenate([c_ctx[None, :], c, jnp.zeros((-n_cond % 8, d), F32)], axis=0)
    mod_all = _modulation(cond, pp["w_mod"], pp["b_mod"])

    rope_s = _rope_tables(ls)
    ident_p = _identity_tables(min(lp, 256))
    ident_c = _identity_tables(min(past, 256))
    zero_h0 = jnp.zeros((n_groups, bp, 4 * SSM_STATE), F32)

    xp = x_prompt.reshape(bp * lp, d)
    xs = x_sample.reshape(bs * ls, d)
    ckvs, kpes, fins = [], [], []
    for l in range(depth):
        mods = jnp.split(mod_all[l], 6, axis=-1)
        mod_ctx = [m[0:1, None, :] for m in mods]
        mod_lat = [m[1:n_cond, None, :] for m in mods]
        ssm_p = dict(t=t_m, p=p_m, rf=rf_m, rb=rb_m, a=a_m, h0=zero_h0)
        xp, ckv, kpe, fin = _block(xp, mod_ctx, pp, vec, ssm_p, l, bp, lp, ident_p, None)
        ckvs.append(ckv.reshape(bp, lp, kv_lora))
        kpes.append(kpe.reshape(bp, lp, ROPE_DIM))
        fins.append(fin)
        sr, si = state_ssm_re[:, l], state_ssm_im[:, l]
        h0 = jnp.concatenate([sr[:, 0], sr[:, 1], si[:, 0], si[:, 1]], axis=-1).transpose(1, 0, 2)
        kpe_ctx = jnp.pad(cache_kpe[:, l].reshape(bs * past, ROPE_DIM), ((0, 0), (0, ROPE_DIM)))
        cache = (cache_ckv[:, l].reshape(bs * past, kv_lora), kpe_ctx) + ident_c
        ssm_s = dict(t=t_m, p=p_m, rf=rf_m, rb=rb_m, a=a_m, h0=h0)
        xs = _block(xs, mod_lat, pp, vec, ssm_s, l, bs, ls, rope_s, cache)[0]

    new_ckv = jnp.stack(ckvs, axis=1)
    new_kpe = jnp.stack(kpes, axis=1)
    fin = jnp.stack(fins, axis=0)
    fin = fin.reshape(depth, n_groups, bp, 2, 2, SSM_STATE).transpose(3, 2, 0, 4, 1, 5)
    return (xp.reshape(bp, lp, d), xs.reshape(bs, ls, d), new_ckv, new_kpe, fin[0], fin[1])
```
